```python
import math
import jax
import jax.numpy as jnp
from jax import lax
import numpy as np

D_MODEL = 1024
BATCH = 2
SEQ = 8192
DEPTH = 4
DEC_BATCH = 128
DEC_SEQ = 1
PAST_LEN = 8192
PAGE_SIZE = 128

M_HEADDIM = 64
M_HEADS = D_MODEL // M_HEADDIM
M_INNER = M_HEADS * M_HEADDIM
M_GROUPS = 4
M_STATE = 128
CONV_W = 4
CONV_DIM = M_INNER + 2 * M_GROUPS * M_STATE
CHUNK = 128
DT_MIN = 0.001
DT_MAX = 0.1
S5_WIDTH = D_MODEL
S5_GSIZE = 16
S5_GROUPS = S5_WIDTH // S5_GSIZE
S5_STATE = 64
HEAD_DIM = 64
A_HEADS = D_MODEL // HEAD_DIM
KV_HEADS = 4
Q_PER_KV = A_HEADS // KV_HEADS
ROT_DIM = HEAD_DIM // 4
ROPE_THETA = 500000.0
WINDOW = 128
D_FF = 4 * D_MODEL
N_BRANCH = 3
EPS = 1e-6
WIN_BUF = min(WINDOW, PAST_LEN)

OFF_Z = 0
OFF_XBC = OFF_Z + M_INNER
OFF_DT = OFF_XBC + CONV_DIM
OFF_U = OFF_DT + M_HEADS
OFF_Q = OFF_U + S5_WIDTH
OFF_K = OFF_Q + A_HEADS * HEAD_DIM
OFF_V = OFF_K + KV_HEADS * HEAD_DIM
OFF_G = OFF_V + KV_HEADS * HEAD_DIM
IN_COLS = OFF_G + N_BRANCH * D_MODEL

kernel_name = 'hybrid_ssd_s5_swa_decoder_step'


def _rmsnorm(x, w):
    xf = x.astype(jnp.float32)
    y = xf * lax.rsqrt(jnp.mean(xf * xf, axis=-1, keepdims=True) + EPS)
    return (y * w.astype(jnp.float32)).astype(x.dtype)


def _rope(x, pos):
    half = ROT_DIM // 2
    inv_freq = jnp.exp(-(2.0 * jnp.arange(half, dtype=jnp.float32) / ROT_DIM) * math.log(ROPE_THETA))
    ang = pos.astype(jnp.float32)[:, None] * inv_freq[None, :]
    cos = jnp.cos(ang)[None, :, None, :]
    sin = jnp.sin(ang)[None, :, None, :]
    xf = x.astype(jnp.float32)
    x1 = xf[..., :half]
    x2 = xf[..., half:ROT_DIM]
    out = jnp.concatenate([x1 * cos - x2 * sin, x2 * cos + x1 * sin, xf[..., ROT_DIM:]], axis=-1)
    return out.astype(x.dtype)


def _causal_conv(xbc, buf, conv_w, conv_b):
    full = jnp.concatenate([buf.astype(xbc.dtype), xbc], axis=1)
    out = lax.conv_general_dilated(full, conv_w[:, None, :].astype(xbc.dtype), window_strides=(1,),
                                   padding='VALID', dimension_numbers=('NWC', 'WIO', 'NWC'),
                                   feature_group_count=CONV_DIM)
    return jax.nn.silu(out + conv_b.astype(xbc.dtype)), full[:, -(CONV_W - 1):]


def _ssd(x, dt, a_log, bm, cm, d_skip, h0):
    bsz, L = x.shape[0], x.shape[1]
    q = CHUNK if L % CHUNK == 0 else L
    nc = L // q
    hpg = M_HEADS // M_GROUPS
    A = -jnp.exp(a_log).reshape(M_GROUPS, hpg)
    xc = x.reshape(bsz, nc, q, M_GROUPS, hpg, M_HEADDIM)
    dtc = dt.reshape(bsz, nc, q, M_GROUPS, hpg)
    bc = bm.reshape(bsz, nc, q, M_GROUPS, M_STATE)
    cc = cm.reshape(bsz, nc, q, M_GROUPS, M_STATE)
    acum = jnp.cumsum(dtc * A, axis=2)
    acum_h = jnp.moveaxis(acum, 2, -1)
    seg = acum_h[..., :, None] - acum_h[..., None, :]
    causal = jnp.tril(jnp.ones((q, q), dtype=bool))
    decay = jnp.exp(jnp.where(causal, seg, -jnp.inf))
    cb = jnp.einsum('bctgn,bcsgn->bcgts', cc, bc)
    xdt = xc * dtc[..., None]
    y_diag = jnp.einsum('bcghts,bcsghp->bctghp', cb[:, :, :, None] * decay, xdt)
    decay_end = jnp.exp(acum[:, :, -1:] - acum)
    states = jnp.einsum('bcsgn,bcsgh,bcsghp->bcghpn', bc, decay_end * dtc, xc)
    chunk_decay = jnp.exp(acum[:, :, -1])

    def step(h, inp):
        st, dec = inp
        return h * dec[..., None, None] + st, h

    h0g = h0.reshape(bsz, M_GROUPS, hpg, M_HEADDIM, M_STATE)
    h_last, h_prev = lax.scan(step, h0g, (jnp.moveaxis(states, 1, 0), jnp.moveaxis(chunk_decay, 1, 0)))
    h_prev = jnp.moveaxis(h_prev, 0, 1)
    y_off = jnp.einsum('bctgn,bcghpn,bctgh->bctghp', cc, h_prev, jnp.exp(acum))
    y = y_diag + y_off + xc * d_skip.reshape(M_GROUPS, hpg)[:, :, None]
    return y.reshape(bsz, L, M_HEADS, M_HEADDIM), h_last.reshape(bsz, M_HEADS, M_HEADDIM, M_STATE)


def _mamba2(z, xbc, dt_raw, conv_buf, h0, conv_w, conv_b, dt_bias, a_log, m_d, m_norm_w):
    f32 = jnp.float32
    bsz, L = z.shape[0], z.shape[1]
    xbc_c, conv_new = _causal_conv(xbc, conv_buf, conv_w, conv_b)
    xbc_c = xbc_c.astype(f32)
    gn = M_GROUPS * M_STATE
    xs = xbc_c[..., :M_INNER].reshape(bsz, L, M_HEADS, M_HEADDIM)
    bm = xbc_c[..., M_INNER:M_INNER + gn].reshape(bsz, L, M_GROUPS, M_STATE)
    cm = xbc_c[..., M_INNER + gn:].reshape(bsz, L, M_GROUPS, M_STATE)
    dt = jax.nn.softplus(dt_raw.astype(f32) + dt_bias.astype(f32))
    y, h_last = _ssd(xs, dt, a_log.astype(f32), bm, cm, m_d.astype(f32), h0.astype(f32))
    y = y.reshape(bsz, L, M_INNER) * jax.nn.silu(z.astype(f32))
    yg = y.reshape(bsz, L, M_GROUPS, M_INNER // M_GROUPS)
    yg = yg * lax.rsqrt(jnp.mean(yg * yg, axis=-1, keepdims=True) + EPS)
    y = yg.reshape(bsz, L, M_INNER) * m_norm_w.astype(f32)
    return y.astype(z.dtype), conv_new, h_last.astype(h0.dtype)


def _s5(u, h0_re, h0_im, lam_re, lam_im, log_step, b_re, b_im, c_re, c_im, d_skip):
    f32 = jnp.float32
    bsz, L = u.shape[0], u.shape[1]
    uf = u.astype(f32).reshape(bsz, L, S5_GROUPS, S5_GSIZE)
    step = jnp.exp(log_step.astype(f32))[:, None]
    lr = lam_re.astype(f32)
    li = lam_im.astype(f32)
    mag = jnp.exp(lr * step)
    ab_re = mag * jnp.cos(li * step)
    ab_im = mag * jnp.sin(li * step)
    den = lr * lr + li * li
    nr = ab_re - 1.0
    ni = ab_im
    f_re = (nr * lr + ni * li) / den
    f_im = (ni * lr - nr * li) / den
    br = b_re.astype(f32)
    bi = b_im.astype(f32)
    bb_re = f_re[..., None] * br - f_im[..., None] * bi
    bb_im = f_re[..., None] * bi + f_im[..., None] * br
    bu_re = jnp.einsum('blgi,gni->blgn', uf, bb_re)
    bu_im = jnp.einsum('blgi,gni->blgn', uf, bb_im)
    a_re = jnp.broadcast_to(ab_re[None, None], (1, L, S5_GROUPS, S5_STATE))
    a_im = jnp.broadcast_to(ab_im[None, None], (1, L, S5_GROUPS, S5_STATE))

    def combine(left, right):
        ar1, ai1, br1, bi1 = left
        ar2, ai2, br2, bi2 = right
        return (ar2 * ar1 - ai2 * ai1, ar2 * ai1 + ai2 * ar1,
                ar2 * br1 - ai2 * bi1 + br2, ar2 * bi1 + ai2 * br1 + bi2)

    p_re, p_im, s_re, s_im = lax.associative_scan(combine, (a_re, a_im, bu_re, bu_im), axis=1)
    hr0 = h0_re.astype(f32)[:, None]
    hi0 = h0_im.astype(f32)[:, None]
    h_re = p_re * hr0 - p_im * hi0 + s_re
    h_im = p_re * hi0 + p_im * hr0 + s_im
    y = (jnp.einsum('gon,blgn->blgo', c_re.astype(f32), h_re)
         - jnp.einsum('gon,blgn->blgo', c_im.astype(f32), h_im))
    y = y + d_skip.astype(f32).reshape(S5_GROUPS, S5_GSIZE) * uf
    y = jax.nn.gelu(y).reshape(bsz, L, S5_WIDTH)
    return y.astype(u.dtype), h_re[:, -1].astype(h0_re.dtype), h_im[:, -1].astype(h0_im.dtype)


def _sink_attend(q, k, v, q_pos, k_pos, sinks):
    s = jnp.einsum('bntkgd,bnskd->bnkgts', q.astype(jnp.float32), k.astype(jnp.float32)) * (HEAD_DIM ** -0.5)
    dpos = q_pos[:, :, None] - k_pos[:, None, :]
    valid = (dpos >= 0) & (dpos <= WINDOW) & (k_pos[:, None, :] >= 0)
    s = jnp.where(valid[None, :, None, None], s, -jnp.inf)
    sink = sinks.astype(jnp.float32).reshape(KV_HEADS, Q_PER_KV)[None, None, :, :, None, None]
    m = jnp.maximum(jnp.max(s, axis=-1, keepdims=True), sink)
    p = jnp.exp(s - m)
    denom = jnp.sum(p, axis=-1, keepdims=True) + jnp.exp(sink - m)
    return jnp.einsum('bnkgts,bnskd->bntkgd', p / denom, v.astype(jnp.float32))


def _attn_prompt(q, k, v, sinks):
    bsz, L = q.shape[0], q.shape[1]
    nb = L // WINDOW
    qb = q.reshape(bsz, nb, WINDOW, KV_HEADS, Q_PER_KV, HEAD_DIM)
    kb = k.reshape(bsz, nb, WINDOW, KV_HEADS, HEAD_DIM)
    vb = v.reshape(bsz, nb, WINDOW, KV_HEADS, HEAD_DIM)
    pad = jnp.zeros_like(kb[:, :1])
    k_ctx = jnp.concatenate([jnp.concatenate([pad, kb[:, :-1]], axis=1), kb], axis=2)
    v_ctx = jnp.concatenate([jnp.concatenate([pad, vb[:, :-1]], axis=1), vb], axis=2)
    q_pos = jnp.arange(L, dtype=jnp.int32).reshape(nb, WINDOW)
    k_pos = (jnp.arange(nb, dtype=jnp.int32)[:, None] - 1) * WINDOW + jnp.arange(2 * WINDOW, dtype=jnp.int32)[None, :]
    o = _sink_attend(qb, k_ctx, v_ctx, q_pos, k_pos, sinks)
    return o.reshape(bsz, L, A_HEADS * HEAD_DIM).astype(q.dtype)


def _attn_sample(q, k, v, k_buf, v_buf, sinks, pos0):
    bsz, T = q.shape[0], q.shape[1]
    k_all = jnp.concatenate([k_buf.astype(k.dtype), k], axis=1)
    v_all = jnp.concatenate([v_buf.astype(v.dtype), v], axis=1)
    qb = q.reshape(bsz, 1, T, KV_HEADS, Q_PER_KV, HEAD_DIM)
    q_pos = (pos0 + jnp.arange(T, dtype=jnp.int32))[None]
    k_pos = (pos0 - WIN_BUF + jnp.arange(WIN_BUF + T, dtype=jnp.int32))[None]
    o = _sink_attend(qb, k_all[:, None], v_all[:, None], q_pos, k_pos, sinks)
    return (o.reshape(bsz, T, A_HEADS * HEAD_DIM).astype(q.dtype),
            k_all[:, -WIN_BUF:], v_all[:, -WIN_BUF:])


def _layer(x, pos0, conv_buf, ssm_h0, s5_h0r, s5_h0i, k_buf, v_buf,
           norm1_w, w_in, conv_w, conv_b, dt_bias, a_log, m_d, m_norm_w, m_proj,
           s5_lam_re, s5_lam_im, s5_log_step, s5_b_re, s5_b_im, s5_c_re, s5_c_im, s5_d, s5_glu_w,
           attn_sinks, attn_o, w_out, norm2_w, mlp_up, mlp_down):
    bsz, L, _ = x.shape
    dty = x.dtype
    h = _rmsnorm(x, norm1_w)
    proj = h @ w_in.astype(dty)
    z = proj[..., OFF_Z:OFF_XBC]
    xbc = proj[..., OFF_XBC:OFF_DT]
    dt_raw = proj[..., OFF_DT:OFF_U]
    u = proj[..., OFF_U:OFF_Q]
    q_raw = proj[..., OFF_Q:OFF_K]
    k_raw = proj[..., OFF_K:OFF_V]
    v_raw = proj[..., OFF_V:OFF_G]
    g_pre = proj[..., OFF_G:]

    y_m, conv_new, ssm_new = _mamba2(z, xbc, dt_raw, conv_buf, ssm_h0, conv_w, conv_b,
                                     dt_bias, a_log, m_d, m_norm_w)
    y_m = y_m @ m_proj.astype(dty)

    y_s, s5r_new, s5i_new = _s5(u, s5_h0r, s5_h0i, s5_lam_re, s5_lam_im, s5_log_step,
                                s5_b_re, s5_b_im, s5_c_re, s5_c_im, s5_d)
    glu = y_s @ s5_glu_w.astype(dty)
    y_s = glu[..., :D_MODEL] * jax.nn.sigmoid(glu[..., D_MODEL:])

    pos = pos0 + jnp.arange(L, dtype=jnp.int32)
    q = _rope(q_raw.reshape(bsz, L, A_HEADS, HEAD_DIM), pos)
    k = _rope(k_raw.reshape(bsz, L, KV_HEADS, HEAD_DIM), pos)
    v = v_raw.reshape(bsz, L, KV_HEADS, HEAD_DIM)
    if k_buf is None:
        o = _attn_prompt(q, k, v, attn_sinks)
        k_new = k[:, -WIN_BUF:]
        v_new = v[:, -WIN_BUF:]
    else:
        o, k_new, v_new = _attn_sample(q, k, v, k_buf, v_buf, attn_sinks, pos0)
    y_a = o @ attn_o.astype(dty)

    gates = jax.nn.sigmoid(g_pre.astype(jnp.float32)).astype(dty).reshape(bsz, L, N_BRANCH, D_MODEL)
    merged = gates[..., 0, :] * y_m + gates[..., 1, :] * y_s + gates[..., 2, :] * y_a
    x = x + merged @ w_out.astype(dty)

    h2 = _rmsnorm(x, norm2_w)
    x = x + jnp.square(jax.nn.relu(h2 @ mlp_up.astype(dty))) @ mlp_down.astype(dty)
    return x, conv_new, ssm_new, s5r_new, s5i_new, k_new, v_new


def setup_inputs(seed: int = 0) -> dict:
    key = jax.random.key(seed)
    ks = jax.random.split(key, 40)
    f32 = jnp.float32

    def nrm(k, shape, scale):
        return jax.random.normal(k, shape, f32) * scale

    x_prompt = nrm(ks[0], (BATCH, SEQ, D_MODEL), 1.0)
    x_sample = nrm(ks[1], (DEC_BATCH, DEC_SEQ, D_MODEL), 1.0)
    state_ssm = nrm(ks[2], (DEPTH, DEC_BATCH, M_HEADS, M_HEADDIM, M_STATE), 0.5)
    state_conv = nrm(ks[3], (DEPTH, DEC_BATCH, CONV_W - 1, CONV_DIM), 1.0)
    state_s5_re = nrm(ks[4], (DEPTH, DEC_BATCH, S5_GROUPS, S5_STATE), 0.1)
    state_s5_im = nrm(ks[5], (DEPTH, DEC_BATCH, S5_GROUPS, S5_STATE), 0.1)
    cache_k = nrm(ks[6], (DEPTH, DEC_BATCH, WIN_BUF, KV_HEADS, HEAD_DIM), 1.0)
    cache_v = nrm(ks[7], (DEPTH, DEC_BATCH, WIN_BUF, KV_HEADS, HEAD_DIM), 1.0)

    norm1_w = 1.0 + nrm(ks[8], (DEPTH, D_MODEL), 0.01)
    w_in = nrm(ks[9], (DEPTH, D_MODEL, IN_COLS), D_MODEL ** -0.5)
    conv_w = nrm(ks[10], (DEPTH, CONV_W, CONV_DIM), CONV_W ** -0.5)
    conv_b = nrm(ks[11], (DEPTH, CONV_DIM), 0.01)
    dt0 = jnp.exp(jax.random.uniform(ks[12], (DEPTH, M_HEADS), f32, math.log(DT_MIN), math.log(DT_MAX)))
    dt_bias = dt0 + jnp.log(-jnp.expm1(-dt0))
    a_log = jnp.log(jax.random.uniform(ks[13], (DEPTH, M_HEADS), f32, 1.0, 16.0))
    m_d = 1.0 + nrm(ks[14], (DEPTH, M_HEADS), 0.01)
    m_norm_w = 1.0 + nrm(ks[15], (DEPTH, M_INNER), 0.01)
    m_proj = nrm(ks[16], (DEPTH, M_INNER, D_MODEL), M_INNER ** -0.5)

    s5_lam_re = -0.5 + nrm(ks[17], (DEPTH, S5_GROUPS, S5_STATE), 0.01)
    s5_lam_im = math.pi * jnp.arange(S5_STATE, dtype=f32) + nrm(ks[18], (DEPTH, S5_GROUPS, S5_STATE), 0.01)
    s5_log_step = jax.random.uniform(ks[19], (DEPTH, S5_GROUPS), f32, math.log(DT_MIN), math.log(DT_MAX))
    s5_b_re = nrm(ks[20], (DEPTH, S5_GROUPS, S5_STATE, S5_GSIZE), (2 * S5_GSIZE) ** -0.5)
    s5_b_im = nrm(ks[21], (DEPTH, S5_GROUPS, S5_STATE, S5_GSIZE), (2 * S5_GSIZE) ** -0.5)
    s5_c_re = nrm(ks[22], (DEPTH, S5_GROUPS, S5_GSIZE, S5_STATE), S5_STATE ** -0.5)
    s5_c_im = nrm(ks[23], (DEPTH, S5_GROUPS, S5_GSIZE, S5_STATE), S5_STATE ** -0.5)
    s5_d = nrm(ks[24], (DEPTH, S5_WIDTH), 1.0)
    s5_glu_w = nrm(ks[25], (DEPTH, S5_WIDTH, 2 * D_MODEL), S5_WIDTH ** -0.5)

    attn_sinks = nrm(ks[26], (DEPTH, A_HEADS), 1.0)
    attn_o = nrm(ks[27], (DEPTH, A_HEADS * HEAD_DIM, D_MODEL), (A_HEADS * HEAD_DIM) ** -0.5)
    w_out = nrm(ks[28], (DEPTH, D_MODEL, D_MODEL), D_MODEL ** -0.5)
    norm2_w = 1.0 + nrm(ks[29], (DEPTH, D_MODEL), 0.01)
    mlp_up = nrm(ks[30], (DEPTH, D_MODEL, D_FF), D_MODEL ** -0.5)
    mlp_down = nrm(ks[31], (DEPTH, D_FF, D_MODEL), D_FF ** -0.5)
    final_norm_w = 1.0 + nrm(ks[32], (D_MODEL,), 0.01)

    return {
        'x_prompt': x_prompt, 'x_sample': x_sample,
        'state_ssm': state_ssm, 'state_conv': state_conv,
        'state_s5_re': state_s5_re, 'state_s5_im': state_s5_im,
        'cache_k': cache_k, 'cache_v': cache_v,
        'norm1_w': norm1_w, 'w_in': w_in, 'conv_w': conv_w, 'conv_b': conv_b,
        'dt_bias': dt_bias, 'a_log': a_log, 'm_d': m_d, 'm_norm_w': m_norm_w, 'm_proj': m_proj,
        's5_lam_re': s5_lam_re, 's5_lam_im': s5_lam_im, 's5_log_step': s5_log_step,
        's5_b_re': s5_b_re, 's5_b_im': s5_b_im, 's5_c_re': s5_c_re, 's5_c_im': s5_c_im,
        's5_d': s5_d, 's5_glu_w': s5_glu_w,
        'attn_sinks': attn_sinks, 'attn_o': attn_o, 'w_out': w_out,
        'norm2_w': norm2_w, 'mlp_up': mlp_up, 'mlp_down': mlp_down,
        'final_norm_w': final_norm_w,
    }


def reference(x_prompt, x_sample, state_ssm, state_conv, state_s5_re, state_s5_im, cache_k, cache_v,
              norm1_w, w_in, conv_w, conv_b, dt_bias, a_log, m_d, m_norm_w, m_proj,
              s5_lam_re, s5_lam_im, s5_log_step, s5_b_re, s5_b_im, s5_c_re, s5_c_im, s5_d, s5_glu_w,
              attn_sinks, attn_o, w_out, norm2_w, mlp_up, mlp_down, final_norm_w):
    xp = x_prompt
    xs = x_sample
    bp = xp.shape[0]
    dty = xp.dtype
    conv0 = jnp.zeros((bp, CONV_W - 1, CONV_DIM), dty)
    ssm0 = jnp.zeros((bp, M_HEADS, M_HEADDIM, M_STATE), dty)
    s50 = jnp.zeros((bp, S5_GROUPS, S5_STATE), dty)
    conv_p, conv_s, ssm_p, ssm_s = [], [], [], []
    s5re_p, s5re_s, s5im_p, s5im_s = [], [], [], []
    k_p, k_s, v_p, v_s = [], [], [], []
    for l in range(DEPTH):
        lp = (norm1_w[l], w_in[l], conv_w[l], conv_b[l], dt_bias[l], a_log[l], m_d[l], m_norm_w[l], m_proj[l],
              s5_lam_re[l], s5_lam_im[l], s5_log_step[l], s5_b_re[l], s5_b_im[l], s5_c_re[l], s5_c_im[l],
              s5_d[l], s5_glu_w[l], attn_sinks[l], attn_o[l], w_out[l], norm2_w[l], mlp_up[l], mlp_down[l])
        xp, c_n, h_n, r_n, i_n, kk, vv = _layer(xp, 0, conv0, ssm0, s50, s50, None, None, *lp)
        conv_p.append(c_n)
        ssm_p.append(h_n)
        s5re_p.append(r_n)
        s5im_p.append(i_n)
        k_p.append(kk)
        v_p.append(vv)
        xs, c_n, h_n, r_n, i_n, kk, vv = _layer(xs, PAST_LEN, state_conv[l], state_ssm[l], state_s5_re[l],
                                                state_s5_im[l], cache_k[l], cache_v[l], *lp)
        conv_s.append(c_n)
        ssm_s.append(h_n)
        s5re_s.append(r_n)
        s5im_s.append(i_n)
        k_s.append(kk)
        v_s.append(vv)
    y_prompt = _rmsnorm(xp, final_norm_w)
    y_sample = _rmsnorm(xs, final_norm_w)
    return (y_prompt, y_sample,
            jnp.stack(ssm_p), jnp.stack(ssm_s), jnp.stack(conv_p), jnp.stack(conv_s),
            jnp.stack(s5re_p), jnp.stack(s5re_s), jnp.stack(s5im_p), jnp.stack(s5im_s),
            jnp.stack(k_p), jnp.stack(k_s), jnp.stack(v_p), jnp.stack(v_s))
```

```python
import functools
import math

import jax
import jax.numpy as jnp
from jax import lax
from jax.experimental import pallas as pl
from jax.experimental.pallas import tpu as pltpu

F32 = jnp.float32
BF16 = jnp.bfloat16

D_MODEL = 1024
DEPTH = 4
PAST_LEN = 8192
M_HEADDIM = 64
M_HEADS = 16
M_INNER = 1024
M_GROUPS = 4
M_STATE = 128
CONV_W = 4
CONV_DIM = 2048
CHUNK = 128
S5_GROUPS = 64
S5_GSIZE = 16
S5_STATE = 64
HEAD_DIM = 64
A_HEADS = 16
KV_HEADS = 4
Q_PER_KV = 4
ROT_DIM = 16
ROPE_THETA = 500000.0
WINDOW = 128
D_FF = 4096
EPS = 1e-6

OFF_Z, OFF_XBC, OFF_DT, OFF_U, OFF_Q, OFF_K, OFF_V, OFF_G, IN_COLS = 0, 1024, 3072, 3088, 4112, 5136, 5392, 5648, 8720

LANE = 128
SUBLANE = 8
S5_LB = 8
S5_LBN = S5_LB * S5_STATE
VMEM_LIMIT = 56 * 1024 * 1024

P_Z, P_X, P_BC, P_U, P_Q, P_G, P_K, P_V, P_COLS = 0, 1024, 2048, 3072, 4096, 5120, 8192, 8448, 8704
S_Z, S_X, S_BC, S_U, S_QX, S_G, S_K, S_V, S_COLS = 0, 1024, 2048, 3072, 4096, 8192, 11264, 11520, 11776
QX = A_HEADS * KV_HEADS * HEAD_DIM


def _sds(shape, dtype=F32):
    return jax.ShapeDtypeStruct(shape, dtype)


def _cparams(n_axes):
    return pltpu.CompilerParams(dimension_semantics=("arbitrary",) * n_axes, vmem_limit_bytes=VMEM_LIMIT)


def _sigmoid(x):
    return 1.0 / (1.0 + jnp.exp(-x))


def _silu(x):
    return x * _sigmoid(x)


def _softplus(x):
    return jnp.maximum(x, 0.0) + jnp.log(1.0 + jnp.exp(-jnp.abs(x)))


def _gelu_tanh(x):
    return 0.5 * x * (1.0 + jnp.tanh(math.sqrt(2.0 / math.pi) * (x + 0.044715 * (x * x * x))))


def _bdot(a, b):
    return jnp.dot(a.astype(BF16), b.astype(BF16), preferred_element_type=F32)


def _split3(a):
    hi = a.astype(BF16)
    r = a - hi.astype(F32)
    mid = r.astype(BF16)
    lo = (r - mid.astype(F32)).astype(BF16)
    return hi, mid, lo


def _dot3_right(a, sel):
    hi, mid, lo = _split3(a)
    return (jnp.dot(hi, sel, preferred_element_type=F32) + jnp.dot(mid, sel, preferred_element_type=F32)
            + jnp.dot(lo, sel, preferred_element_type=F32))


def _dot3_left(sel, a):
    hi, mid, lo = _split3(a)
    return (jnp.dot(sel, hi, preferred_element_type=F32) + jnp.dot(sel, mid, preferred_element_type=F32)
            + jnp.dot(sel, lo, preferred_element_type=F32))


def _rms(x, w):
    return x * lax.rsqrt(jnp.mean(x * x, axis=-1, keepdims=True) + EPS) * w


def _norm_proj_body(x_ref, nw_ref, w_ref, wdt_ref, o_ref, odt_ref, h_scr):
    @pl.when(pl.program_id(1) == 0)
    def _():
        hb = _rms(x_ref[...], nw_ref[...]).astype(BF16)
        h_scr[...] = hb
        odt_ref[...] = jnp.dot(hb, wdt_ref[...], preferred_element_type=F32)

    o_ref[...] = jnp.dot(h_scr[...], w_ref[...], preferred_element_type=F32)


def _norm_proj(x, nw, w, wdt, tm, tn):
    m, d = x.shape
    n = w.shape[1]
    return pl.pallas_call(
        _norm_proj_body,
        grid=(m // tm, n // tn),
        in_specs=[pl.BlockSpec((tm, d), lambda i, j: (i, 0)),
                  pl.BlockSpec((1, d), lambda i, j: (0, 0)),
                  pl.BlockSpec((d, tn), lambda i, j: (0, j)),
                  pl.BlockSpec((d, LANE), lambda i, j: (0, 0))],
        out_specs=[pl.BlockSpec((tm, tn), lambda i, j: (i, j)),
                   pl.BlockSpec((tm, LANE), lambda i, j: (i, 0))],
        out_shape=[_sds((m, n)), _sds((m, LANE))],
        scratch_shapes=[pltpu.VMEM((tm, d), BF16)],
        compiler_params=_cparams(2),
        name="norm_proj",
    )(x, nw, w, wdt)


def _merge_mlp_body(final, x_ref, g0_ref, g1_ref, g2_ref, ym_ref, ys_ref, ya_ref, glu_ref, wo_ref, n2_ref, up_ref,
                    dn_ref, fn_ref, o_ref):
    glu = jnp.dot(ys_ref[...].astype(BF16), glu_ref[...], preferred_element_type=F32)
    y_s = glu[:, :D_MODEL] * _sigmoid(glu[:, D_MODEL:])
    merged = (_sigmoid(g0_ref[...]) * ym_ref[...] + _sigmoid(g1_ref[...]) * y_s
              + _sigmoid(g2_ref[...]) * ya_ref[...])
    x1 = x_ref[...] + jnp.dot(merged.astype(BF16), wo_ref[...], preferred_element_type=F32)
    h2 = _rms(x1, n2_ref[...]).astype(BF16)
    acc = x1
    fc = 1024
    for c in range(D_FF // fc):
        a = jnp.dot(h2, up_ref[:, c * fc:(c + 1) * fc], preferred_element_type=F32)
        a = jnp.square(jnp.maximum(a, 0.0))
        acc = acc + jnp.dot(a.astype(BF16), dn_ref[c * fc:(c + 1) * fc, :], preferred_element_type=F32)
    o_ref[...] = _rms(acc, fn_ref[...]) if final else acc


def _merge_mlp(x, proj, g_col, ym, ys, ya, glu_w, w_out, n2, up, dn, fnw, tm, final):
    m, d = x.shape
    gb = g_col // d
    row = lambda i: (i, 0)
    const = lambda i: (0, 0)
    wspec = lambda shape: pl.BlockSpec(shape, const, pipeline_mode=pl.Buffered(1))
    return pl.pallas_call(
        functools.partial(_merge_mlp_body, final),
        grid=(m // tm,),
        in_specs=[pl.BlockSpec((tm, d), row),
                  pl.BlockSpec((tm, d), lambda i: (i, gb)),
                  pl.BlockSpec((tm, d), lambda i: (i, gb + 1)),
                  pl.BlockSpec((tm, d), lambda i: (i, gb + 2)),
                  pl.BlockSpec((tm, d), row), pl.BlockSpec((tm, d), row), pl.BlockSpec((tm, d), row),
                  wspec((d, 2 * d)), wspec((d, d)), wspec((1, d)), wspec((d, D_FF)), wspec((D_FF, d)),
                  wspec((1, d))],
        out_specs=pl.BlockSpec((tm, d), row),
        out_shape=_sds((m, d)),
        compiler_params=_cparams(1),
        name="merge_mlp",
    )(x, proj, proj, proj, ym, ys, ya, glu_w, w_out, n2, up, dn, fnw)


def _group_norm_gate(y, z, nw):
    y = y * _silu(z)
    gw = M_INNER // M_GROUPS
    parts = []
    for g in range(M_GROUPS):
        yg = y[:, g * gw:(g + 1) * gw]
        parts.append(yg * lax.rsqrt(jnp.mean(yg * yg, axis=-1, keepdims=True) + EPS))
    return jnp.concatenate(parts, axis=1) * nw


def _mamba_prompt_body(z_ref, x_ref, bc_ref, dt_ref, cw_ref, cb_ref, dtb_ref, alog_ref, dskip_ref, nw_ref,
                       exp_ref, mproj_ref, y_ref, conv_ref, ssm_ref, xpad, ht):
    c = pl.program_id(1)
    nc = pl.num_programs(1)
    q = CHUNK
    pad = SUBLANE

    @pl.when(c == 0)
    def _():
        xpad[0:pad, :] = jnp.zeros((pad, CONV_DIM), F32)
        ht[...] = jnp.zeros(ht.shape, F32)

    xraw = x_ref[...]
    bcraw = bc_ref[...]
    xpad[pad:pad + q, 0:M_INNER] = xraw
    xpad[pad:pad + q, M_INNER:CONV_DIM] = bcraw
    conv = cb_ref[...] + cw_ref[0:1, :] * xpad[pad - 3:pad - 3 + q, :]
    for k in range(1, CONV_W):
        conv = conv + cw_ref[k:k + 1, :] * xpad[pad - 3 + k:pad - 3 + k + q, :]
    xpad[0:pad, :] = xpad[q:q + pad, :]

    @pl.when(c == nc - 1)
    def _():
        conv_ref[:, 0:M_INNER] = xraw[q - 3:q, :]
        conv_ref[:, M_INNER:CONV_DIM] = bcraw[q - 3:q, :]

    xbc = _silu(conv)
    xs = xbc[:, 0:M_INNER]
    gn = M_GROUPS * M_STATE
    bm = xbc[:, M_INNER:M_INNER + gn]
    cm = xbc[:, M_INNER + gn:]

    dt = _softplus(dt_ref[...] + dtb_ref[...])
    a = dt * (-jnp.exp(alog_ref[...]))
    ri = lax.broadcasted_iota(jnp.int32, (q, q), 0)
    ci = lax.broadcasted_iota(jnp.int32, (q, q), 1)
    causal = ci <= ri
    tri = causal.astype(BF16)
    acum = _dot3_left(tri, a)
    expand = exp_ref[...]
    acum_x = _dot3_right(acum, expand)
    dt_x = _dot3_right(dt, expand)
    acum_last = acum_x[q - 1:q, :]
    xdt = xs * dt_x
    xw = xs * (jnp.exp(acum_last - acum_x) * dt_x)
    e_acum = jnp.exp(acum_x)
    chunk_decay = jnp.exp(acum_last)
    acum_t = acum.T

    hpg = M_HEADS // M_GROUPS
    gw = hpg * M_HEADDIM
    y_parts = []
    for g in range(M_GROUPS):
        cg = cm[:, g * M_STATE:(g + 1) * M_STATE].astype(BF16)
        bg = bm[:, g * M_STATE:(g + 1) * M_STATE]
        bgb = bg.astype(BF16)
        cb = lax.dot_general(cg, bgb, (((1,), (1,)), ((), ())), preferred_element_type=F32)
        h_prev = ht[g]
        y_off = jnp.dot(cg, h_prev.astype(BF16), preferred_element_type=F32) * e_acum[:, g * gw:(g + 1) * gw]
        yd = []
        for hl in range(hpg):
            h = g * hpg + hl
            seg = acum[:, h:h + 1] - acum_t[h:h + 1, :]
            lmat = jnp.exp(jnp.where(causal, seg, -jnp.inf))
            mh = (cb * lmat).astype(BF16)
            yd.append(jnp.dot(mh, xdt[:, h * M_HEADDIM:(h + 1) * M_HEADDIM].astype(BF16),
                              preferred_element_type=F32))
        y_parts.append(jnp.concatenate(yd, axis=1) + y_off)
        s_new = jnp.dot(bg.T.astype(BF16), xw[:, g * gw:(g + 1) * gw].astype(BF16), preferred_element_type=F32)
        ht[g] = h_prev * chunk_decay[:, g * gw:(g + 1) * gw] + s_new
    y = jnp.concatenate(y_parts, axis=1) + xs * dskip_ref[...]
    y = _group_norm_gate(y, z_ref[...], nw_ref[...])
    y_ref[...] = jnp.dot(y.astype(BF16), mproj_ref[...], preferred_element_type=F32)

    @pl.when(c == nc - 1)
    def _():
        for g in range(M_GROUPS):
            ssm_ref[g * hpg:(g + 1) * hpg] = ht[g].T.reshape(hpg, M_HEADDIM, M_STATE)


def _mamba_prompt(proj, dtp, conv_w, conv_b, dt_bias, a_log, d_skip, m_norm_w, expand, m_proj):
    b, l, _ = proj.shape
    d = D_MODEL
    nc = l // CHUNK
    const2 = lambda i, j: (0, 0)
    return pl.pallas_call(
        _mamba_prompt_body,
        grid=(b, nc),
        in_specs=[pl.BlockSpec((None, CHUNK, d), lambda i, j: (i, j, P_Z // d)),
                  pl.BlockSpec((None, CHUNK, d), lambda i, j: (i, j, P_X // d)),
                  pl.BlockSpec((None, CHUNK, d), lambda i, j: (i, j, P_BC // d)),
                  pl.BlockSpec((None, CHUNK, LANE), lambda i, j: (i, j, 0)),
                  pl.BlockSpec((CONV_W, CONV_DIM), const2),
                  pl.BlockSpec((1, CONV_DIM), const2),
                  pl.BlockSpec((1, LANE), const2),
                  pl.BlockSpec((1, LANE), const2),
                  pl.BlockSpec((1, d), const2),
                  pl.BlockSpec((1, d), const2),
                  pl.BlockSpec((LANE, d), const2),
                  pl.BlockSpec((d, d), const2)],
        out_specs=[pl.BlockSpec((None, CHUNK, d), lambda i, j: (i, j, 0)),
                   pl.BlockSpec((None, CONV_W - 1, CONV_DIM), lambda i, j: (i, 0, 0)),
                   pl.BlockSpec((None, M_HEADS, M_HEADDIM, M_STATE), lambda i, j: (i, 0, 0, 0))],
        out_shape=[_sds((b, l, d)), _sds((b, CONV_W - 1, CONV_DIM)), _sds((b, M_HEADS, M_HEADDIM, M_STATE))],
        scratch_shapes=[pltpu.VMEM((SUBLANE + CHUNK, CONV_DIM), F32),
                        pltpu.VMEM((M_GROUPS, M_STATE, (M_HEADS // M_GROUPS) * M_HEADDIM), F32)],
        compiler_params=_cparams(2),
        name="mamba_prompt",
    )(proj, proj, proj, dtp, conv_w, conv_b, dt_bias, a_log, d_skip, m_norm_w, expand, m_proj)


def _s5_disc_body(lr_ref, li_ref, ls_ref, bre_ref, bim_ref, pre_ref, pim_ref, bbre_ref, bbim_ref):
    lr = lr_ref[...]
    li = li_ref[...]
    step = jnp.exp(ls_ref[...])
    mag = jnp.exp(lr * step)
    ab_re = mag * jnp.cos(li * step)
    ab_im = mag * jnp.sin(li * step)
    den = lr * lr + li * li
    nr = ab_re - 1.0
    ni = ab_im
    f_re = (nr * lr + ni * li) / den
    f_im = (ni * lr - nr * li) / den
    for i in range(S5_GSIZE):
        bbre_ref[i] = f_re * bre_ref[i] - f_im * bim_ref[i]
        bbim_ref[i] = f_re * bim_ref[i] + f_im * bre_ref[i]
    p_re, p_im = ab_re, ab_im
    pre_ref[0] = p_re
    pim_ref[0] = p_im
    for k in range(1, SUBLANE):
        p_re, p_im = p_re * ab_re - p_im * ab_im, p_re * ab_im + p_im * ab_re
        pre_ref[k] = p_re
        pim_ref[k] = p_im


def _s5_disc(lam_re, lam_im, log_step, b_re_t, b_im_t):
    g, n = lam_re.shape
    return pl.pallas_call(
        _s5_disc_body,
        out_shape=[_sds((SUBLANE, g, n)), _sds((SUBLANE, g, n)), _sds((S5_GSIZE, g, n)), _sds((S5_GSIZE, g, n))],
        name="s5_disc",
    )(lam_re, lam_im, log_step, b_re_t, b_im_t)


def _s5_tables(pw_re, pw_im, bb_re, bb_im, c_re, c_im):
    nb = S5_GROUPS // S5_LB
    eye = jnp.eye(S5_LB, dtype=F32)

    def bd_in(bb):
        t = bb.transpose(1, 0, 2).reshape(nb, S5_LB, S5_GSIZE, S5_STATE)
        return jnp.einsum('bgin,gh->bgihn', t, eye).reshape(nb, LANE, S5_LBN).astype(BF16)

    def bd_out(cc):
        t = cc.reshape(nb, S5_LB, S5_GSIZE, S5_STATE)
        return jnp.einsum('bgon,gh->bgnho', t, eye).reshape(nb, S5_LBN, LANE).astype(BF16)

    def lanes(p):
        return p.reshape(SUBLANE, nb, S5_LBN).transpose(1, 0, 2)

    rows = jnp.arange(SUBLANE)[None, :, None]
    pr, pi = lanes(pw_re), lanes(pw_im)
    tabs = [pr, pi]
    for d in (1, 2, 4):
        for p in (pr, pi):
            tabs.append(jnp.where(rows >= d, p[:, d - 1:d, :], 0.0))
    tab = jnp.stack(tabs, axis=1)
    return bd_in(bb_re), bd_in(bb_im), bd_out(c_re), bd_out(-c_im), tab


def _s5_scan_tile(xr, xi, tab_ref, car_re, car_im):
    for d, k in ((1, 2), (2, 4), (4, 6)):
        ar = tab_ref[k]
        ai = tab_ref[k + 1]
        sr = pltpu.roll(xr, d, 0)
        si = pltpu.roll(xi, d, 0)
        xr, xi = xr + ar * sr - ai * si, xi + ar * si + ai * sr
    pr = tab_ref[0]
    pi = tab_ref[1]
    hr = xr + pr * car_re - pi * car_im
    hi = xi + pr * car_im + pi * car_re
    return hr, hi


def _s5_prompt_body(u_ref, wbr_ref, wbi_ref, wcr_ref, wci_ref, tab_ref, d_ref, y_ref, hre_ref, him_ref, cre, cim):
    c = pl.program_id(2)
    nc = pl.num_programs(2)
    tc = u_ref.shape[0]

    @pl.when(c == 0)
    def _():
        cre[...] = jnp.zeros(cre.shape, F32)
        cim[...] = jnp.zeros(cim.shape, F32)

    def sub(i, carry):
        r0 = pl.multiple_of(i * CHUNK, CHUNK)
        u = u_ref[pl.ds(r0, CHUNK), :]
        ub = u.astype(BF16)
        bre = jnp.dot(ub, wbr_ref[...], preferred_element_type=F32)
        bim = jnp.dot(ub, wbi_ref[...], preferred_element_type=F32)
        car_re = cre[...]
        car_im = cim[...]
        hrs, his = [], []
        for t in range(CHUNK // SUBLANE):
            hr, hi = _s5_scan_tile(bre[t * SUBLANE:(t + 1) * SUBLANE], bim[t * SUBLANE:(t + 1) * SUBLANE],
                                   tab_ref, car_re, car_im)
            car_re = jnp.broadcast_to(hr[SUBLANE - 1:SUBLANE, :], hr.shape)
            car_im = jnp.broadcast_to(hi[SUBLANE - 1:SUBLANE, :], hi.shape)
            hrs.append(hr)
            his.append(hi)
        cre[...] = car_re
        cim[...] = car_im
        hre = jnp.concatenate(hrs, axis=0).astype(BF16)
        him = jnp.concatenate(his, axis=0).astype(BF16)
        y = (jnp.dot(hre, wcr_ref[...], preferred_element_type=F32)
             + jnp.dot(him, wci_ref[...], preferred_element_type=F32) + d_ref[...] * u)
        y_ref[pl.ds(r0, CHUNK), :] = _gelu_tanh(y)
        return carry

    lax.fori_loop(0, tc // CHUNK, sub, 0)

    @pl.when(c == nc - 1)
    def _():
        hre_ref[...] = cre[...]
        him_ref[...] = cim[...]


def _s5_prompt(proj, wbr, wbi, wcr, wci, tab, d_skip, tc):
    b, l, _ = proj.shape
    nb = S5_GROUPS // S5_LB
    ub = P_U // LANE
    wmap = lambda i, g, j: (g, 0, 0)
    return pl.pallas_call(
        _s5_prompt_body,
        grid=(b, nb, l // tc),
        in_specs=[pl.BlockSpec((None, tc, LANE), lambda i, g, j: (i, j, ub + g)),
                  pl.BlockSpec((None, LANE, S5_LBN), wmap), pl.BlockSpec((None, LANE, S5_LBN), wmap),
                  pl.BlockSpec((None, S5_LBN, LANE), wmap), pl.BlockSpec((None, S5_LBN, LANE), wmap),
                  pl.BlockSpec((None, 8, SUBLANE, S5_LBN), lambda i, g, j: (g, 0, 0, 0)),
                  pl.BlockSpec((1, LANE), lambda i, g, j: (0, g))],
        out_specs=[pl.BlockSpec((None, tc, LANE), lambda i, g, j: (i, j, g)),
                   pl.BlockSpec((None, None, SUBLANE, S5_LBN), lambda i, g, j: (i, g, 0, 0)),
                   pl.BlockSpec((None, None, SUBLANE, S5_LBN), lambda i, g, j: (i, g, 0, 0))],
        out_shape=[_sds((b, l, D_MODEL)), _sds((b, nb, SUBLANE, S5_LBN)), _sds((b, nb, SUBLANE, S5_LBN))],
        scratch_shapes=[pltpu.VMEM((SUBLANE, S5_LBN), F32), pltpu.VMEM((SUBLANE, S5_LBN), F32)],
        compiler_params=_cparams(3),
        name="s5_prompt",
    )(proj, wbr, wbi, wcr, wci, tab, d_skip)


def _rope_tables(pos):
    half = ROT_DIM // 2
    inv_freq = jnp.exp(-(2.0 * jnp.arange(half, dtype=F32) / ROT_DIM) * math.log(ROPE_THETA))
    ang = pos.astype(F32)[:, None] * inv_freq[None, :]
    cos, sin = jnp.cos(ang), jnp.sin(ang)
    l = pos.shape[0]
    one = jnp.ones((l, HEAD_DIM - ROT_DIM), F32)
    zero = jnp.zeros((l, HEAD_DIM - ROT_DIM), F32)
    zh = jnp.zeros((l, half), F32)
    cos_h = jnp.concatenate([cos, cos, one], axis=1)
    sa_h = jnp.concatenate([-sin, zh, zero], axis=1)
    sb_h = jnp.concatenate([zh, sin, zero], axis=1)
    two = lambda t: jnp.concatenate([t, t], axis=1)
    return two(cos_h), two(sa_h), two(sb_h)


def _rope_flat(x, cos, sa, sb):
    w = x.shape[1]
    n = w // LANE
    tile = lambda t: jnp.concatenate([t] * n, axis=1) if n > 1 else t
    half = ROT_DIM // 2
    return x * tile(cos) + pltpu.roll(x, w - half, 1) * tile(sa) + pltpu.roll(x, half, 1) * tile(sb)


def _attn_prompt_body(q_ref, k_ref, v_ref, cos_ref, sa_ref, sb_ref, sink_ref, wo_ref, y_ref, kn_ref, vn_ref,
                      kprev, vprev):
    j = pl.program_id(1)
    nb = pl.num_programs(1)
    w = WINDOW
    kvw = KV_HEADS * HEAD_DIM

    @pl.when(j == 0)
    def _():
        kprev[...] = jnp.zeros(kprev.shape, F32)
        vprev[...] = jnp.zeros(vprev.shape, F32)

    cos, sa, sb = cos_ref[...], sa_ref[...], sb_ref[...]
    q = _rope_flat(q_ref[...], cos, sa, sb) * (HEAD_DIM ** -0.5)
    k = _rope_flat(k_ref[...], cos, sa, sb)
    v = v_ref[...]
    kctx = jnp.concatenate([kprev[...], k], axis=0).astype(BF16)
    vctx = jnp.concatenate([vprev[...], v], axis=0).astype(BF16)
    rows = Q_PER_KV * w
    ti = lax.broadcasted_iota(jnp.int32, (rows, 2 * w), 0) % w
    ci = lax.broadcasted_iota(jnp.int32, (rows, 2 * w), 1)
    valid = (ci >= ti) & (ci <= ti + w) & ((ci >= w) | (j > 0))
    outs = []
    for kv in range(KV_HEADS):
        kj = kctx[:, kv * HEAD_DIM:(kv + 1) * HEAD_DIM]
        vj = vctx[:, kv * HEAD_DIM:(kv + 1) * HEAD_DIM]
        heads = range(kv * Q_PER_KV, (kv + 1) * Q_PER_KV)
        qs = jnp.concatenate([q[:, h * HEAD_DIM:(h + 1) * HEAD_DIM] for h in heads], axis=0).astype(BF16)
        sink = jnp.concatenate([jnp.broadcast_to(sink_ref[0:1, h:h + 1], (w, 1)) for h in heads], axis=0)
        s = lax.dot_general(qs, kj, (((1,), (1,)), ((), ())), preferred_element_type=F32)
        s = jnp.where(valid, s, -jnp.inf)
        m = jnp.maximum(jnp.max(s, axis=-1, keepdims=True), sink)
        p = jnp.exp(s - m)
        den = jnp.sum(p, axis=-1, keepdims=True) + jnp.exp(sink - m)
        o = jnp.dot((p / den).astype(BF16), vj, preferred_element_type=F32)
        outs.extend(o[i * w:(i + 1) * w] for i in range(Q_PER_KV))
    o = jnp.concatenate(outs, axis=1)
    y_ref[...] = jnp.dot(o.astype(BF16), wo_ref[...], preferred_element_type=F32)
    kprev[...] = k
    vprev[...] = v

    @pl.when(j == nb - 1)
    def _():
        kn_ref[...] = k
        vn_ref[...] = v


def _attn_prompt(proj, cos, sa, sb, sinks, attn_o):
    b, l, _ = proj.shape
    d = D_MODEL
    w = WINDOW
    kvw = KV_HEADS * HEAD_DIM
    const2 = lambda i, j: (0, 0)
    tmap = lambda i, j: (j, 0)
    return pl.pallas_call(
        _attn_prompt_body,
        grid=(b, l // w),
        in_specs=[pl.BlockSpec((None, w, d), lambda i, j: (i, j, P_Q // d)),
                  pl.BlockSpec((None, w, kvw), lambda i, j: (i, j, P_K // kvw)),
                  pl.BlockSpec((None, w, kvw), lambda i, j: (i, j, P_V // kvw)),
                  pl.BlockSpec((w, LANE), tmap), pl.BlockSpec((w, LANE), tmap), pl.BlockSpec((w, LANE), tmap),
                  pl.BlockSpec((1, LANE), const2),
                  pl.BlockSpec((d, d), const2)],
        out_specs=[pl.BlockSpec((None, w, d), lambda i, j: (i, j, 0)),
                   pl.BlockSpec((None, w, kvw), lambda i, j: (i, 0, 0)),
                   pl.BlockSpec((None, w, kvw), lambda i, j: (i, 0, 0))],
        out_shape=[_sds((b, l, d)), _sds((b, w, kvw)), _sds((b, w, kvw))],
        scratch_shapes=[pltpu.VMEM((w, kvw), F32), pltpu.VMEM((w, kvw), F32)],
        compiler_params=_cparams(2),
        name="attn_prompt",
    )(proj, proj, proj, cos, sa, sb, sinks, attn_o)


SAMPLE_TB = 8


def _mamba_sample_body(x_ref, bc_ref, dt_ref, cst_ref, ssm_ref, cw_ref, cb_ref, dtb_ref, alog_ref, dskip_ref,
                       exp_ref, y_ref, cnew_ref, ssmo_ref):
    tb = SAMPLE_TB
    new = jnp.concatenate([x_ref[...], bc_ref[...]], axis=1)
    conv = (cb_ref[...] + cw_ref[0:1, :] * cst_ref[0] + cw_ref[1:2, :] * cst_ref[1] + cw_ref[2:3, :] * cst_ref[2]
            + cw_ref[3:4, :] * new)
    cnew_ref[0] = cst_ref[1]
    cnew_ref[1] = cst_ref[2]
    cnew_ref[2] = new
    xbc = _silu(conv)
    xs = xbc[:, 0:M_INNER]
    gn = M_GROUPS * M_STATE
    bm = xbc[:, M_INNER:M_INNER + gn]
    cm = xbc[:, M_INNER + gn:].astype(BF16)
    dt = _softplus(dt_ref[...] + dtb_ref[...])
    a = dt * (-jnp.exp(alog_ref[...]))
    expand = exp_ref[...]
    dtx = _dot3_right(dt, expand) * xs
    da_x = jnp.exp(_dot3_right(a, expand))
    nl = M_INNER // LANE
    slab = jnp.concatenate([dtx[:, j * LANE:(j + 1) * LANE] for j in range(nl)]
                           + [da_x[:, j * LANE:(j + 1) * LANE] for j in range(nl)], axis=0)
    tt = slab.T
    hpg = M_HEADS // M_GROUPS
    gw = hpg * M_HEADDIM
    rowid = lax.broadcasted_iota(jnp.int32, (tb, gw), 0)
    ys = []
    for g in range(M_GROUPS):
        yacc = jnp.zeros((tb, gw), F32)
        for t in range(tb):
            parts = []
            for hl in range(hpg):
                h = g * hpg + hl
                j, hh = divmod(h * M_HEADDIM, LANE)
                col = j * tb + t
                dcol = tt[hh:hh + M_HEADDIM, col:col + 1]
                acol = tt[hh:hh + M_HEADDIM, nl * tb + col:nl * tb + col + 1]
                hn = acol * ssm_ref[t, h] + dcol * bm[t:t + 1, g * M_STATE:(g + 1) * M_STATE]
                ssmo_ref[t, h] = hn
                parts.append(hn)
            hng = jnp.concatenate(parts, axis=0).astype(BF16)
            yg = lax.dot_general(cm[:, g * M_STATE:(g + 1) * M_STATE], hng, (((1,), (1,)), ((), ())),
                                 preferred_element_type=F32)
            yacc = jnp.where(rowid == t, yg, yacc)
        ys.append(yacc)
    y_ref[...] = jnp.concatenate(ys, axis=1) + xs * dskip_ref[...]


def _mamba_sample(proj, dtp, conv_t, ssm, conv_w, conv_b, dt_bias, a_log, d_skip, expand):
    s = proj.shape[0]
    d = D_MODEL
    tb = SAMPLE_TB
    const = lambda i: (0, 0)
    return pl.pallas_call(
        _mamba_sample_body,
        grid=(s // tb,),
        in_specs=[pl.BlockSpec((tb, d), lambda i: (i, S_X // d)),
                  pl.BlockSpec((tb, d), lambda i: (i, S_BC // d)),
                  pl.BlockSpec((tb, LANE), lambda i: (i, 0)),
                  pl.BlockSpec((CONV_W - 1, tb, CONV_DIM), lambda i: (0, i, 0)),
                  pl.BlockSpec((tb, M_HEADS, M_HEADDIM, M_STATE), lambda i: (i, 0, 0, 0)),
                  pl.BlockSpec((CONV_W, CONV_DIM), const), pl.BlockSpec((1, CONV_DIM), const),
                  pl.BlockSpec((1, LANE), const), pl.BlockSpec((1, LANE), const), pl.BlockSpec((1, d), const),
                  pl.BlockSpec((LANE, d), const)],
        out_specs=[pl.BlockSpec((tb, d), lambda i: (i, 0)),
                   pl.BlockSpec((CONV_W - 1, tb, CONV_DIM), lambda i: (0, i, 0)),
                   pl.BlockSpec((tb, M_HEADS, M_HEADDIM, M_STATE), lambda i: (i, 0, 0, 0))],
        out_shape=[_sds((s, d)), _sds((CONV_W - 1, s, CONV_DIM)), _sds((s, M_HEADS, M_HEADDIM, M_STATE))],
        compiler_params=_cparams(1),
        name="mamba_sample",
    )(proj, proj, dtp, conv_t, ssm, conv_w, conv_b, dt_bias, a_log, d_skip, expand)


def _gate_norm_proj_body(y_ref, z_ref, nw_ref, w_ref, o_ref):
    y = _group_norm_gate(y_ref[...], z_ref[...], nw_ref[...])
    o_ref[...] = jnp.dot(y.astype(BF16), w_ref[...], preferred_element_type=F32)


def _gate_norm_proj(y, proj, m_norm_w, m_proj):
    s, d = y.shape
    const = lambda i: (0, 0)
    return pl.pallas_call(
        _gate_norm_proj_body,
        grid=(1,),
        in_specs=[pl.BlockSpec((s, d), const), pl.BlockSpec((s, d), lambda i: (0, S_Z // d)),
                  pl.BlockSpec((1, d), const), pl.BlockSpec((d, d), const)],
        out_specs=pl.BlockSpec((s, d), const),
        out_shape=_sds((s, d)),
        compiler_params=_cparams(1),
        name="gate_norm_proj",
    )(y, proj, m_norm_w, m_proj)


def _s5_sample_body(u_ref, hre_ref, him_ref, wbr_ref, wbi_ref, wcr_ref, wci_ref, tab_ref, d_ref, y_ref, ore_ref,
                    oim_ref):
    u = u_ref[...]
    ub = u.astype(BF16)
    bre = jnp.dot(ub, wbr_ref[...], preferred_element_type=F32)
    bim = jnp.dot(ub, wbi_ref[...], preferred_element_type=F32)
    ar = tab_ref[0][0:1, :]
    ai = tab_ref[1][0:1, :]
    h0r = hre_ref[...]
    h0i = him_ref[...]
    hr = ar * h0r - ai * h0i + bre
    hi = ar * h0i + ai * h0r + bim
    ore_ref[...] = hr
    oim_ref[...] = hi
    y = (jnp.dot(hr.astype(BF16), wcr_ref[...], preferred_element_type=F32)
         + jnp.dot(hi.astype(BF16), wci_ref[...], preferred_element_type=F32) + d_ref[...] * u)
    y_ref[...] = _gelu_tanh(y)


def _s5_sample(proj, h_re, h_im, wbr, wbi, wcr, wci, tab, d_skip):
    s = proj.shape[0]
    nb = S5_GROUPS // S5_LB
    ub = S_U // LANE
    wmap = lambda g: (g, 0, 0)
    hspec = pl.BlockSpec((s, S5_LBN), lambda g: (0, g))
    return pl.pallas_call(
        _s5_sample_body,
        grid=(nb,),
        in_specs=[pl.BlockSpec((s, LANE), lambda g: (0, ub + g)), hspec, hspec,
                  pl.BlockSpec((None, LANE, S5_LBN), wmap), pl.BlockSpec((None, LANE, S5_LBN), wmap),
                  pl.BlockSpec((None, S5_LBN, LANE), wmap), pl.BlockSpec((None, S5_LBN, LANE), wmap),
                  pl.BlockSpec((None, 8, SUBLANE, S5_LBN), lambda g: (g, 0, 0, 0)),
                  pl.BlockSpec((1, LANE), lambda g: (0, g))],
        out_specs=[pl.BlockSpec((s, LANE), lambda g: (0, g)), hspec, hspec],
        out_shape=[_sds((s, D_MODEL)), _sds(h_re.shape), _sds(h_im.shape)],
        compiler_params=_cparams(1),
        name="s5_sample",
    )(proj, h_re, h_im, wbr, wbi, wcr, wci, tab, d_skip)


def _rope_sample_body(q_ref, k_ref, cos_ref, sa_ref, sb_ref, qo_ref, ko_ref):
    cos, sa, sb = cos_ref[...], sa_ref[...], sb_ref[...]
    qo_ref[...] = _rope_flat(q_ref[...], cos, sa, sb) * (HEAD_DIM ** -0.5)
    ko_ref[...] = _rope_flat(k_ref[...], cos, sa, sb)


def _rope_sample(proj, cos, sa, sb):
    s = proj.shape[0]
    kvw = KV_HEADS * HEAD_DIM
    const = lambda i: (0, 0)
    return pl.pallas_call(
        _rope_sample_body,
        grid=(1,),
        in_specs=[pl.BlockSpec((s, QX), lambda i: (0, S_QX // QX)), pl.BlockSpec((s, kvw), lambda i: (0, S_K // kvw)),
                  pl.BlockSpec((1, LANE), const), pl.BlockSpec((1, LANE), const), pl.BlockSpec((1, LANE), const)],
        out_specs=[pl.BlockSpec((s, QX), const), pl.BlockSpec((s, kvw), const)],
        out_shape=[_sds((s, QX)), _sds((s, kvw))],
        compiler_params=_cparams(1),
        name="rope_sample",
    )(proj, proj, cos, sa, sb)


def _attn_sample_body(q_ref, kc_ref, vc_ref, kn_ref, vn_ref, sink_ref, o_ref, ko_ref, vo_ref):
    q = q_ref[...]
    kc = kc_ref[...]
    vc = vc_ref[...]
    kn = kn_ref[...]
    vn = vn_ref[...]
    w = kc.shape[1]
    s_c = lax.dot_general(q.astype(BF16), kc.astype(BF16), (((2,), (2,)), ((0,), (0,))),
                          preferred_element_type=F32)
    s_n = jnp.sum(q * kn, axis=-1, keepdims=True)
    sink = sink_ref[...][None]
    m = jnp.maximum(jnp.maximum(jnp.max(s_c, axis=-1, keepdims=True), s_n), sink)
    p_c = jnp.exp(s_c - m)
    p_n = jnp.exp(s_n - m)
    den = jnp.sum(p_c, axis=-1, keepdims=True) + p_n + jnp.exp(sink - m)
    o = lax.dot_general((p_c / den).astype(BF16), vc.astype(BF16), (((2,), (1,)), ((0,), (0,))),
                        preferred_element_type=F32)
    o_ref[...] = o + (p_n / den) * vn
    ko_ref[:, 0:w - 1, :] = kc[:, 1:w, :]
    ko_ref[:, w - 1:w, :] = kn
    vo_ref[:, 0:w - 1, :] = vc[:, 1:w, :]
    vo_ref[:, w - 1:w, :] = vn


def _attn_sample(qx, kc, vc, kn, proj3, sinks):
    s, w, kvw = kc.shape
    tb = SAMPLE_TB
    row3 = lambda i: (i, 0, 0)
    return pl.pallas_call(
        _attn_sample_body,
        grid=(s // tb,),
        in_specs=[pl.BlockSpec((tb, A_HEADS, kvw), row3), pl.BlockSpec((tb, w, kvw), row3),
                  pl.BlockSpec((tb, w, kvw), row3), pl.BlockSpec((tb, 1, kvw), row3),
                  pl.BlockSpec((tb, 1, kvw), lambda i: (i, 0, S_V // kvw)),
                  pl.BlockSpec((A_HEADS, 1), lambda i: (0, 0))],
        out_specs=[pl.BlockSpec((tb, A_HEADS, kvw), row3), pl.BlockSpec((tb, w, kvw), row3),
                   pl.BlockSpec((tb, w, kvw), row3)],
        out_shape=[_sds((s, A_HEADS, kvw)), _sds((s, w, kvw)), _sds((s, w, kvw))],
        compiler_params=_cparams(1),
        name="attn_sample",
    )(qx, kc, vc, kn, proj3, sinks)


def _dense_body(x_ref, w_ref, o_ref):
    o_ref[...] = jnp.dot(x_ref[...].astype(BF16), w_ref[...], preferred_element_type=F32)


def _dense(x, w, tn):
    m, k = x.shape
    n = w.shape[1]
    return pl.pallas_call(
        _dense_body,
        grid=(n // tn,),
        in_specs=[pl.BlockSpec((m, k), lambda j: (0, 0)), pl.BlockSpec((k, tn), lambda j: (0, j))],
        out_specs=pl.BlockSpec((m, tn), lambda j: (0, j)),
        out_shape=_sds((m, n)),
        compiler_params=_cparams(1),
        name="dense",
    )(x, w)


def _pad_lanes(v):
    return jnp.zeros((1, LANE), F32).at[0, :v.shape[0]].set(v.astype(F32))


def _layer_weights(l, w_in, attn_o):
    w = w_in[l]
    seg = lambda a, b: w[:, a:b]
    z, xbc, dt, u = seg(OFF_Z, OFF_XBC), seg(OFF_XBC, OFF_DT), seg(OFF_DT, OFF_U), seg(OFF_U, OFF_Q)
    q, k, v, g = seg(OFF_Q, OFF_K), seg(OFF_K, OFF_V), seg(OFF_V, OFF_G), seg(OFF_G, IN_COLS)
    w_prompt = jnp.concatenate([z, xbc, u, q, g, k, v], axis=1).astype(BF16)
    sel = (jnp.arange(A_HEADS)[:, None] // Q_PER_KV == jnp.arange(KV_HEADS)[None, :]).astype(F32)
    qx = (q.reshape(D_MODEL, A_HEADS, 1, HEAD_DIM) * sel[None, :, :, None]).reshape(D_MODEL, QX)
    w_sample = jnp.concatenate([z, xbc, u, qx, g, k, v], axis=1).astype(BF16)
    w_dt = jnp.zeros((D_MODEL, LANE), F32).at[:, :M_HEADS].set(dt).astype(BF16)
    ao = attn_o[l].reshape(A_HEADS, 1, HEAD_DIM, D_MODEL) * sel[:, :, None, None]
    return w_prompt, w_sample, w_dt, ao.reshape(QX, D_MODEL).astype(BF16)


def kernel(x_prompt, x_sample, state_ssm, state_conv, state_s5_re, state_s5_im, cache_k, cache_v, norm1_w, w_in,
           conv_w, conv_b, dt_bias, a_log, m_d, m_norm_w, m_proj, s5_lam_re, s5_lam_im, s5_log_step, s5_b_re,
           s5_b_im, s5_c_re, s5_c_im, s5_d, s5_glu_w, attn_sinks, attn_o, w_out, norm2_w, mlp_up, mlp_down,
           final_norm_w):
    b, l, d = x_prompt.shape
    s = x_sample.shape[0]
    kvw = KV_HEADS * HEAD_DIM
    nb = S5_GROUPS // S5_LB
    xp = x_prompt.reshape(b * l, d)
    xs = x_sample.reshape(s, d)
    cos_p, sa_p, sb_p = _rope_tables(jnp.arange(l, dtype=jnp.int32))
    cos_s, sa_s, sb_s = _rope_tables(jnp.full((1,), PAST_LEN, jnp.int32))
    expand = (jnp.arange(LANE)[:, None] == jnp.arange(M_INNER)[None, :] // M_HEADDIM).astype(BF16)
    fnw = final_norm_w[None]
    outs = [[] for _ in range(12)]
    for li in range(DEPTH):
        final = li == DEPTH - 1
        w_prompt, w_sample, w_dt, ao_x = _layer_weights(li, w_in, attn_o)
        n1 = norm1_w[li][None]
        cw, cb = conv_w[li], conv_b[li][None]
        dtb, alog = _pad_lanes(dt_bias[li]), _pad_lanes(a_log[li])
        dsk = jnp.repeat(m_d[li], M_HEADDIM)[None]
        mnw = m_norm_w[li][None]
        mpj = m_proj[li].astype(BF16)
        pw_re, pw_im, bb_re, bb_im = _s5_disc(s5_lam_re[li], s5_lam_im[li], s5_log_step[li][:, None],
                                              s5_b_re[li].transpose(2, 0, 1), s5_b_im[li].transpose(2, 0, 1))
        wbr, wbi, wcr, wci, tab = _s5_tables(pw_re, pw_im, bb_re, bb_im, s5_c_re[li], s5_c_im[li])
        s5d = s5_d[li][None]
        sinks = attn_sinks[li]
        ao = attn_o[li].astype(BF16)
        mlp_w = (s5_glu_w[li].astype(BF16), w_out[li].astype(BF16), norm2_w[li][None], mlp_up[li].astype(BF16),
                 mlp_down[li].astype(BF16), fnw)

        proj, dtp = _norm_proj(xp, n1, w_prompt, w_dt, 1024, 512)
        proj3 = proj.reshape(b, l, P_COLS)
        ym, conv_p, ssm_p = _mamba_prompt(proj3, dtp.reshape(b, l, LANE), cw, cb, dtb, alog, dsk, mnw, expand, mpj)
        ys, hre_p, him_p = _s5_prompt(proj3, wbr, wbi, wcr, wci, tab, s5d, 512)
        ya, k_p, v_p = _attn_prompt(proj3, cos_p, sa_p, sb_p, _pad_lanes(sinks), ao)
        xp = _merge_mlp(xp, proj, P_G, ym.reshape(b * l, d), ys.reshape(b * l, d), ya.reshape(b * l, d), *mlp_w,
                        256, final)

        sproj, sdt = _norm_proj(xs, n1, w_sample, w_dt, s, 512)
        y_pre, conv_t, ssm_s = _mamba_sample(sproj, sdt, state_conv[li].transpose(1, 0, 2), state_ssm[li], cw, cb,
                                             dtb, alog, dsk, expand)
        sym = _gate_norm_proj(y_pre, sproj, mnw, mpj)
        sys_, hre_s, him_s = _s5_sample(sproj, state_s5_re[li].reshape(s, S5_GROUPS * S5_STATE),
                                        state_s5_im[li].reshape(s, S5_GROUPS * S5_STATE), wbr, wbi, wcr, wci, tab, s5d)
        qx_rot, k_rot = _rope_sample(sproj, cos_s, sa_s, sb_s)
        o, k_s, v_s = _attn_sample(qx_rot.reshape(s, A_HEADS, kvw), cache_k[li].reshape(s, WINDOW, kvw),
                                   cache_v[li].reshape(s, WINDOW, kvw), k_rot.reshape(s, 1, kvw),
                                   sproj.reshape(s, 1, S_COLS), sinks[:, None])
        sya = _dense(o.reshape(s, QX), ao_x, 512)
        xs = _merge_mlp(xs, sproj, S_G, sym, sys_, sya, *mlp_w, s, final)

        for i, val in enumerate((
                ssm_p, ssm_s, conv_p, conv_t.transpose(1, 0, 2),
                hre_p[:, :, 0].reshape(b, S5_GROUPS, S5_STATE), hre_s.reshape(s, S5_GROUPS, S5_STATE),
                him_p[:, :, 0].reshape(b, S5_GROUPS, S5_STATE), him_s.reshape(s, S5_GROUPS, S5_STATE),
                k_p.reshape(b, WINDOW, KV_HEADS, HEAD_DIM), k_s.reshape(s, WINDOW, KV_HEADS, HEAD_DIM),
                v_p.reshape(b, WINDOW, KV_HEADS, HEAD_DIM), v_s.reshape(s, WINDOW, KV_HEADS, HEAD_DIM))):
            outs[i].append(val)
    return (xp.reshape(b, l, d), xs.reshape(s, 1, d)) + tuple(jnp.stack(o) for o in outs)
```

```python
import functools
import math

import jax
import jax.numpy as jnp
from jax import lax
from jax.experimental import pallas as pl
from jax.experimental.pallas import tpu as pltpu

F32 = jnp.float32
BF16 = jnp.bfloat16

D_MODEL = 1024
DEPTH = 4
PAST_LEN = 8192
M_HEADDIM = 64
M_HEADS = 16
M_INNER = 1024
M_GROUPS = 4
M_STATE = 128
CONV_W = 4
CONV_DIM = 2048
CHUNK = 128
S5_GROUPS = 64
S5_GSIZE = 16
S5_STATE = 64
HEAD_DIM = 64
A_HEADS = 16
KV_HEADS = 4
Q_PER_KV = 4
ROT_DIM = 16
ROPE_THETA = 500000.0
WINDOW = 128
D_FF = 4096
EPS = 1e-6

OFF_Z, OFF_XBC, OFF_DT, OFF_U, OFF_Q, OFF_K, OFF_V, OFF_G, IN_COLS = 0, 1024, 3072, 3088, 4112, 5136, 5392, 5648, 8720

LANE = 128
SUBLANE = 8
S5_LB = 8
S5_LBN = S5_LB * S5_STATE
VMEM_LIMIT = 56 * 1024 * 1024

P_Z, P_X, P_BC, P_U, P_Q, P_G, P_K, P_V, P_COLS = 0, 1024, 2048, 3072, 4096, 5120, 8192, 8448, 8704
S_Z, S_X, S_BC, S_U, S_QX, S_G, S_K, S_V, S_COLS = 0, 1024, 2048, 3072, 4096, 8192, 11264, 11520, 11776
QX = A_HEADS * KV_HEADS * HEAD_DIM


def _sds(shape, dtype=F32):
    return jax.ShapeDtypeStruct(shape, dtype)


def _cparams(n_axes):
    return pltpu.CompilerParams(dimension_semantics=("arbitrary",) * n_axes, vmem_limit_bytes=VMEM_LIMIT)


def _sigmoid(x):
    return 1.0 / (1.0 + jnp.exp(-x))


def _silu(x):
    return x * _sigmoid(x)


def _softplus(x):
    return jnp.maximum(x, 0.0) + jnp.log(1.0 + jnp.exp(-jnp.abs(x)))


def _gelu_tanh(x):
    return 0.5 * x * (1.0 + jnp.tanh(math.sqrt(2.0 / math.pi) * (x + 0.044715 * (x * x * x))))


def _bdot(a, b):
    return jnp.dot(a.astype(BF16), b.astype(BF16), preferred_element_type=F32)


def _split3(a):
    hi = a.astype(BF16)
    r = a - hi.astype(F32)
    mid = r.astype(BF16)
    lo = (r - mid.astype(F32)).astype(BF16)
    return hi, mid, lo


def _dot3_right(a, sel):
    hi, mid, lo = _split3(a)
    return (jnp.dot(hi, sel, preferred_element_type=F32) + jnp.dot(mid, sel, preferred_element_type=F32)
            + jnp.dot(lo, sel, preferred_element_type=F32))


def _dot3_left(sel, a):
    hi, mid, lo = _split3(a)
    return (jnp.dot(sel, hi, preferred_element_type=F32) + jnp.dot(sel, mid, preferred_element_type=F32)
            + jnp.dot(sel, lo, preferred_element_type=F32))


def _rms(x, w):
    return x * lax.rsqrt(jnp.mean(x * x, axis=-1, keepdims=True) + EPS) * w


def _norm_proj_body(x_ref, nw_ref, w_ref, wdt_ref, o_ref, odt_ref, h_scr):
    @pl.when(pl.program_id(1) == 0)
    def _():
        hb = _rms(x_ref[...], nw_ref[...]).astype(BF16)
        h_scr[...] = hb
        odt_ref[...] = jnp.dot(hb, wdt_ref[...], preferred_element_type=F32)

    o_ref[...] = jnp.dot(h_scr[...], w_ref[...], preferred_element_type=F32).astype(o_ref.dtype)


def _norm_proj(x, nw, w, wdt, tm, tn, out_dtype):
    m, d = x.shape
    n = w.shape[1]
    return pl.pallas_call(
        _norm_proj_body,
        grid=(m // tm, n // tn),
        in_specs=[pl.BlockSpec((tm, d), lambda i, j: (i, 0)),
                  pl.BlockSpec((1, d), lambda i, j: (0, 0)),
                  pl.BlockSpec((d, tn), lambda i, j: (0, j)),
                  pl.BlockSpec((d, LANE), lambda i, j: (0, 0))],
        out_specs=[pl.BlockSpec((tm, tn), lambda i, j: (i, j)),
                   pl.BlockSpec((tm, LANE), lambda i, j: (i, 0))],
        out_shape=[_sds((m, n), out_dtype), _sds((m, LANE))],
        scratch_shapes=[pltpu.VMEM((tm, d), BF16)],
        compiler_params=_cparams(2),
        name="norm_proj",
    )(x, nw, w, wdt)


def _merge_mlp_body(final, x_ref, g0_ref, g1_ref, g2_ref, ym_ref, ys_ref, ya_ref, glu_ref, wo_ref, n2_ref, up_ref,
                    dn_ref, fn_ref, o_ref):
    ys = jnp.concatenate([ys_ref[g] for g in range(ys_ref.shape[0])], axis=1)
    glu = jnp.dot(ys.astype(BF16), glu_ref[...], preferred_element_type=F32)
    y_s = glu[:, :D_MODEL] * _sigmoid(glu[:, D_MODEL:])
    f32 = lambda r: r[...].astype(F32)
    merged = _sigmoid(f32(g0_ref)) * f32(ym_ref) + _sigmoid(f32(g1_ref)) * y_s + _sigmoid(f32(g2_ref)) * f32(ya_ref)
    x1 = x_ref[...] + jnp.dot(merged.astype(BF16), wo_ref[...], preferred_element_type=F32)
    h2 = _rms(x1, n2_ref[...]).astype(BF16)
    acc = x1
    fc = 1024
    for c in range(D_FF // fc):
        a = jnp.dot(h2, up_ref[:, c * fc:(c + 1) * fc], preferred_element_type=F32)
        a = jnp.square(jnp.maximum(a, 0.0))
        acc = acc + jnp.dot(a.astype(BF16), dn_ref[c * fc:(c + 1) * fc, :], preferred_element_type=F32)
    o_ref[...] = _rms(acc, fn_ref[...]) if final else acc


def _merge_mlp(x, proj, g_col, ym, ys, ya, glu_w, w_out, n2, up, dn, fnw, tm, final):
    m, d = x.shape
    gb = g_col // d
    nlb = d // LANE
    row = lambda i: (i, 0)
    const = lambda i: (0, 0)
    wspec = lambda shape: pl.BlockSpec(shape, const, pipeline_mode=pl.Buffered(1))
    return pl.pallas_call(
        functools.partial(_merge_mlp_body, final),
        grid=(m // tm,),
        in_specs=[pl.BlockSpec((tm, d), row),
                  pl.BlockSpec((tm, d), lambda i: (i, gb)),
                  pl.BlockSpec((tm, d), lambda i: (i, gb + 1)),
                  pl.BlockSpec((tm, d), lambda i: (i, gb + 2)),
                  pl.BlockSpec((tm, d), row), pl.BlockSpec((nlb, tm, LANE), lambda i: (0, i, 0)),
                  pl.BlockSpec((tm, d), row),
                  wspec((d, 2 * d)), wspec((d, d)), wspec((1, d)), wspec((d, D_FF)), wspec((D_FF, d)),
                  wspec((1, d))],
        out_specs=pl.BlockSpec((tm, d), row),
        out_shape=_sds((m, d)),
        compiler_params=_cparams(1),
        name="merge_mlp",
    )(x, proj, proj, proj, ym, ys, ya, glu_w, w_out, n2, up, dn, fnw)


def _group_norm_gate(y, z, nw):
    y = y * _silu(z)
    gw = M_INNER // M_GROUPS
    parts = []
    for g in range(M_GROUPS):
        yg = y[:, g * gw:(g + 1) * gw]
        parts.append(yg * lax.rsqrt(jnp.mean(yg * yg, axis=-1, keepdims=True) + EPS))
    return jnp.concatenate(parts, axis=1) * nw


def _mamba_prompt_body(z_ref, x_ref, bc_ref, dt_ref, cw_ref, cb_ref, dtb_ref, alog_ref, dskip_ref, nw_ref,
                       exp_ref, mproj_ref, y_ref, conv_ref, ssm_ref, xprev, ht):
    c = pl.program_id(1)
    nc = pl.num_programs(1)
    q = CHUNK
    tail = xprev.shape[0]

    @pl.when(c == 0)
    def _():
        xprev[...] = jnp.zeros(xprev.shape, xprev.dtype)
        ht[...] = jnp.zeros(ht.shape, F32)

    xcur = jnp.concatenate([x_ref[...], bc_ref[...]], axis=1)
    xext = jnp.concatenate([xprev[...], xcur], axis=0)
    ri = lax.broadcasted_iota(jnp.int32, (q, tail + q), 0)
    ci = lax.broadcasted_iota(jnp.int32, (q, tail + q), 1)
    conv = cb_ref[...] + cw_ref[CONV_W - 1:CONV_W, :] * xcur.astype(F32)
    for k in range(CONV_W - 1):
        shift = (ci == ri + (tail - (CONV_W - 1) + k)).astype(BF16)
        conv = conv + cw_ref[k:k + 1, :] * jnp.dot(shift, xext, preferred_element_type=F32)
    last_rows = xcur[q - tail:q, :]
    xprev[...] = last_rows

    @pl.when(c == nc - 1)
    def _():
        conv_ref[...] = last_rows.astype(F32)[tail - (CONV_W - 1):tail, :]

    xbc = _silu(conv)
    xs = xbc[:, 0:M_INNER]
    gn = M_GROUPS * M_STATE
    bm = xbc[:, M_INNER:M_INNER + gn]
    cm = xbc[:, M_INNER + gn:]

    dt = _softplus(dt_ref[...] + dtb_ref[...])
    a = dt * (-jnp.exp(alog_ref[...]))
    ri = lax.broadcasted_iota(jnp.int32, (q, q), 0)
    ci = lax.broadcasted_iota(jnp.int32, (q, q), 1)
    causal = ci <= ri
    tri = causal.astype(BF16)
    acum = _dot3_left(tri, a)
    expand = exp_ref[...]
    acum_x = _dot3_right(acum, expand)
    dt_x = _dot3_right(dt, expand)
    acum_last = acum_x[q - 1:q, :]
    xdt = xs * dt_x
    xw = xs * (jnp.exp(acum_last - acum_x) * dt_x)
    e_acum = jnp.exp(acum_x)
    chunk_decay = jnp.exp(acum_last)
    acum_t = acum.T

    hpg = M_HEADS // M_GROUPS
    gw = hpg * M_HEADDIM
    y_parts = []
    for g in range(M_GROUPS):
        cg = cm[:, g * M_STATE:(g + 1) * M_STATE].astype(BF16)
        bg = bm[:, g * M_STATE:(g + 1) * M_STATE]
        bgb = bg.astype(BF16)
        cb = lax.dot_general(cg, bgb, (((1,), (1,)), ((), ())), preferred_element_type=F32)
        h_prev = ht[g]
        y_off = jnp.dot(cg, h_prev.astype(BF16), preferred_element_type=F32) * e_acum[:, g * gw:(g + 1) * gw]
        yd = []
        for hl in range(hpg):
            h = g * hpg + hl
            seg = acum[:, h:h + 1] - acum_t[h:h + 1, :]
            lmat = jnp.exp(jnp.where(causal, seg, -jnp.inf))
            mh = (cb * lmat).astype(BF16)
            yd.append(jnp.dot(mh, xdt[:, h * M_HEADDIM:(h + 1) * M_HEADDIM].astype(BF16),
                              preferred_element_type=F32))
        y_parts.append(jnp.concatenate(yd, axis=1) + y_off)
        s_new = jnp.dot(bg.T.astype(BF16), xw[:, g * gw:(g + 1) * gw].astype(BF16), preferred_element_type=F32)
        ht[g] = h_prev * chunk_decay[:, g * gw:(g + 1) * gw] + s_new
    y = jnp.concatenate(y_parts, axis=1) + xs * dskip_ref[...]
    y = _group_norm_gate(y, z_ref[...].astype(F32), nw_ref[...])
    y_ref[...] = jnp.dot(y.astype(BF16), mproj_ref[...], preferred_element_type=F32).astype(y_ref.dtype)

    @pl.when(c == nc - 1)
    def _():
        for g in range(M_GROUPS):
            ssm_ref[g * hpg:(g + 1) * hpg] = ht[g].T.reshape(hpg, M_HEADDIM, M_STATE)


def _mamba_prompt(proj, dtp, conv_w, conv_b, dt_bias, a_log, d_skip, m_norm_w, expand, m_proj):
    b, l, _ = proj.shape
    d = D_MODEL
    nc = l // CHUNK
    const2 = lambda i, j: (0, 0)
    return pl.pallas_call(
        _mamba_prompt_body,
        grid=(b, nc),
        in_specs=[pl.BlockSpec((None, CHUNK, d), lambda i, j: (i, j, P_Z // d)),
                  pl.BlockSpec((None, CHUNK, d), lambda i, j: (i, j, P_X // d)),
                  pl.BlockSpec((None, CHUNK, d), lambda i, j: (i, j, P_BC // d)),
                  pl.BlockSpec((None, CHUNK, LANE), lambda i, j: (i, j, 0)),
                  pl.BlockSpec((CONV_W, CONV_DIM), const2),
                  pl.BlockSpec((1, CONV_DIM), const2),
                  pl.BlockSpec((1, LANE), const2),
                  pl.BlockSpec((1, LANE), const2),
                  pl.BlockSpec((1, d), const2),
                  pl.BlockSpec((1, d), const2),
                  pl.BlockSpec((LANE, d), const2),
                  pl.BlockSpec((d, d), const2)],
        out_specs=[pl.BlockSpec((None, CHUNK, d), lambda i, j: (i, j, 0)),
                   pl.BlockSpec((None, CONV_W - 1, CONV_DIM), lambda i, j: (i, 0, 0)),
                   pl.BlockSpec((None, M_HEADS, M_HEADDIM, M_STATE), lambda i, j: (i, 0, 0, 0))],
        out_shape=[_sds((b, l, d), BF16), _sds((b, CONV_W - 1, CONV_DIM)), _sds((b, M_HEADS, M_HEADDIM, M_STATE))],
        scratch_shapes=[pltpu.VMEM((2 * SUBLANE, CONV_DIM), BF16),
                        pltpu.VMEM((M_GROUPS, M_STATE, (M_HEADS // M_GROUPS) * M_HEADDIM), F32)],
        compiler_params=_cparams(2),
        name="mamba_prompt",
    )(proj, proj, proj, dtp, conv_w, conv_b, dt_bias, a_log, d_skip, m_norm_w, expand, m_proj)


S5_Q = 8


def _cmul(a, b):
    return a[0] * b[0] - a[1] * b[1], a[0] * b[1] + a[1] * b[0]


def _s5_disc_body(lr_ref, li_ref, ls_ref, bre_ref, bim_ref, cre_ref, cim_ref, p1re_ref, p1im_ref, pqre_ref, pqim_ref,
                  bare_ref, baim_ref, care_ref, caim_ref):
    lr = lr_ref[...]
    li = li_ref[...]
    step = jnp.exp(ls_ref[...])
    mag = jnp.exp(lr * step)
    ab = (mag * jnp.cos(li * step), mag * jnp.sin(li * step))
    den = lr * lr + li * li
    nr = ab[0] - 1.0
    ni = ab[1]
    f = ((nr * lr + ni * li) / den, (ni * lr - nr * li) / den)
    p1re_ref[...] = ab[0]
    p1im_ref[...] = ab[1]
    pows = [(jnp.ones_like(lr), jnp.zeros_like(lr))]
    for _ in range(S5_Q):
        pows.append(_cmul(pows[-1], ab))
    r = pows[S5_Q]
    for k in range(SUBLANE):
        pqre_ref[k] = r[0]
        pqim_ref[k] = r[1]
        r = _cmul(r, pows[S5_Q])
    for i in range(S5_GSIZE):
        bb = _cmul(f, (bre_ref[i], bim_ref[i]))
        for s in range(S5_Q):
            v = _cmul(pows[S5_Q - 1 - s], bb)
            bare_ref[s, i] = v[0]
            baim_ref[s, i] = v[1]
        c = (cre_ref[i], cim_ref[i])
        for k in range(S5_Q + 1):
            v = _cmul(c, pows[k])
            care_ref[k, i] = v[0]
            caim_ref[k, i] = v[1]


def _s5_disc(lam_re, lam_im, log_step, b_re_t, b_im_t, c_re_t, c_im_t):
    g, n = lam_re.shape
    gn = _sds((g, n))
    pq = _sds((SUBLANE, g, n))
    ba = _sds((S5_Q, S5_GSIZE, g, n))
    ca = _sds((S5_Q + 1, S5_GSIZE, g, n))
    return pl.pallas_call(
        _s5_disc_body,
        out_shape=[gn, gn, pq, pq, ba, ba, ca, ca],
        name="s5_disc",
    )(lam_re, lam_im, log_step, b_re_t, b_im_t, c_re_t, c_im_t)


def _hi_lo(a):
    hi = a.astype(BF16)
    return hi, (a - hi.astype(F32)).astype(BF16)


def _s5_kmat_body(car_ref, cai_ref, bbr_ref, bbi_ref, k_ref):
    dn = (((2,), (2,)), ((0,), (0,)))

    def mm(a, b):
        ah, al = _hi_lo(a)
        bh, bl = _hi_lo(b)
        d = lambda x, y: lax.dot_general(x, y, dn, preferred_element_type=F32)
        return d(ah, bh) + d(ah, bl) + d(al, bh)

    k_ref[...] = mm(car_ref[...], bbr_ref[...]) - mm(cai_ref[...], bbi_ref[...])


def _s5_kmat(ca_re, ca_im, bb_re, bb_im):
    g, m, _ = ca_re.shape
    return pl.pallas_call(
        _s5_kmat_body,
        out_shape=_sds((g, m, S5_GSIZE)),
        name="s5_kmat",
    )(ca_re, ca_im, bb_re, bb_im)


def _s5_operators(lam_re, lam_im, log_step, b_re, b_im, c_re, c_im):
    nb, q = S5_GROUPS // S5_LB, S5_Q
    p1_re, p1_im, pq_re, pq_im, ba_re, ba_im, ca_re, ca_im = _s5_disc(
        lam_re, lam_im, log_step[:, None], b_re.transpose(2, 0, 1), b_im.transpose(2, 0, 1),
        c_re.transpose(1, 0, 2), c_im.transpose(1, 0, 2))
    rows_k = lambda ca: ca[:q].transpose(2, 0, 1, 3).reshape(S5_GROUPS, q * S5_GSIZE, S5_STATE)
    kmat = _s5_kmat(rows_k(ca_re), rows_k(ca_im), ba_re[q - 1].transpose(1, 0, 2), ba_im[q - 1].transpose(1, 0, 2))
    eye = jnp.eye(S5_LB, dtype=F32)

    k5 = kmat.reshape(nb, S5_LB, q, S5_GSIZE, S5_GSIZE)
    lag = jnp.arange(q)[None, :] - jnp.arange(q)[:, None]
    kk = jnp.where((lag >= 0)[None, None, :, :, None, None], k5[:, :, jnp.clip(lag, 0, q - 1)], 0.0)
    t_in = jnp.einsum('bgstoi,gh->bsgitho', kk, eye).reshape(nb, q * LANE, q * LANE).astype(BF16)
    ba = jnp.stack([ba_re, ba_im], 0).reshape(2, q, S5_GSIZE, nb, S5_LB, S5_STATE)
    w_st = jnp.einsum('csibgn,gh->bsgichn', ba, eye).reshape(nb, q * LANE, 2 * S5_LBN).astype(BF16)
    ca = jnp.stack([ca_re[1:], -ca_im[1:]], 0).reshape(2, q, S5_GSIZE, nb, S5_LB, S5_STATE)
    w_out = jnp.einsum('ctobgn,gh->bcgntho', ca, eye).reshape(nb, 2 * S5_LBN, q * LANE).astype(BF16)

    lanes = lambda p: p.reshape(-1, nb, S5_LBN).transpose(1, 0, 2)
    rows = jnp.arange(SUBLANE)[None, :, None]
    pr, pi = lanes(pq_re), lanes(pq_im)
    tabs = [pr, pi]
    for d in (1, 2, 4):
        for p in (pr, pi):
            tabs.append(jnp.where(rows >= d, p[:, d - 1:d, :], 0.0))
    tab = jnp.stack(tabs, axis=1)

    def bd_in(bb):
        t = bb.transpose(1, 0, 2).reshape(nb, S5_LB, S5_GSIZE, S5_STATE)
        return jnp.einsum('bgin,gh->bgihn', t, eye).reshape(nb, LANE, S5_LBN).astype(BF16)

    def bd_out(cc):
        t = cc.reshape(nb, S5_LB, S5_GSIZE, S5_STATE)
        return jnp.einsum('bgon,gh->bgnho', t, eye).reshape(nb, S5_LBN, LANE).astype(BF16)

    step1 = (bd_in(ba_re[q - 1]), bd_in(ba_im[q - 1]), bd_out(c_re), bd_out(-c_im),
             jnp.stack([lanes(p1_re), lanes(p1_im)], axis=1))
    return (t_in, w_st, w_out, tab), step1


def _s5_scan_tile(xr, xi, tab_ref, car_re, car_im):
    for d, k in ((1, 2), (2, 4), (4, 6)):
        ar = tab_ref[k]
        ai = tab_ref[k + 1]
        sr = pltpu.roll(xr, d, 0)
        si = pltpu.roll(xi, d, 0)
        xr, xi = xr + ar * sr - ai * si, xi + ar * si + ai * sr
    pr = tab_ref[0]
    pi = tab_ref[1]
    hr = xr + pr * car_re - pi * car_im
    hi = xi + pr * car_im + pi * car_re
    return hr, hi


def _s5_prompt_body(*refs):
    q = S5_Q
    x_refs = refs[:q]
    tin_ref, wst_ref, wout_ref, tab_ref, d_ref, y_ref, hfin_ref, car = refs[q:]
    j = pl.program_id(2)
    nj = pl.num_programs(2)
    nrow = x_refs[0].shape[0]

    @pl.when(j == 0)
    def _():
        car[...] = jnp.zeros(car.shape, F32)

    xb = jnp.concatenate([r[...] for r in x_refs], axis=1)
    s = jnp.dot(xb, wst_ref[...], preferred_element_type=F32)
    car_re = car[:, 0:S5_LBN]
    car_im = car[:, S5_LBN:]
    first = lax.broadcasted_iota(jnp.int32, (SUBLANE, S5_LBN), 0) == 0
    hp = []
    for t in range(nrow // SUBLANE):
        rs = slice(t * SUBLANE, (t + 1) * SUBLANE)
        hr, hi = _s5_scan_tile(s[rs, 0:S5_LBN], s[rs, S5_LBN:], tab_ref, car_re, car_im)
        hp.append(jnp.concatenate([jnp.where(first, car_re, pltpu.roll(hr, 1, 0)),
                                   jnp.where(first, car_im, pltpu.roll(hi, 1, 0))], axis=1))
        car_re = jnp.broadcast_to(hr[SUBLANE - 1:SUBLANE, :], hr.shape)
        car_im = jnp.broadcast_to(hi[SUBLANE - 1:SUBLANE, :], hi.shape)
    car[:, 0:S5_LBN] = car_re
    car[:, S5_LBN:] = car_im
    h_in = jnp.concatenate(hp, axis=0).astype(BF16)
    d_row = jnp.concatenate([d_ref[...]] * q, axis=1)
    y = (jnp.dot(xb, tin_ref[...], preferred_element_type=F32)
         + jnp.dot(h_in, wout_ref[...], preferred_element_type=F32) + d_row * xb.astype(F32))
    y_ref[...] = _gelu_tanh(y).astype(y_ref.dtype)

    @pl.when(j == nj - 1)
    def _():
        hfin_ref[...] = car[...]


def _s5_prompt(proj, t_in, w_st, w_out, tab, d_skip, nrow):
    b, l, cols = proj.shape
    q = S5_Q
    nb = S5_GROUPS // S5_LB
    proj_q = proj.reshape(b, l // q, q * cols)
    cb = cols // LANE
    ub = P_U // LANE
    wmap = lambda g, i, j: (g, 0, 0)
    x_specs = [pl.BlockSpec((None, nrow, LANE), functools.partial(lambda g, i, j, t: (i, j, t * cb + ub + g), t=t))
               for t in range(q)]
    y, h_fin = pl.pallas_call(
        _s5_prompt_body,
        grid=(nb, b, l // q // nrow),
        in_specs=x_specs + [pl.BlockSpec((None, q * LANE, q * LANE), wmap),
                            pl.BlockSpec((None, q * LANE, 2 * S5_LBN), wmap),
                            pl.BlockSpec((None, 2 * S5_LBN, q * LANE), wmap),
                            pl.BlockSpec((None, 8, SUBLANE, S5_LBN), lambda g, i, j: (g, 0, 0, 0)),
                            pl.BlockSpec((1, LANE), lambda g, i, j: (0, g))],
        out_specs=[pl.BlockSpec((None, None, nrow, q * LANE), lambda g, i, j: (g, i, j, 0)),
                   pl.BlockSpec((None, None, SUBLANE, 2 * S5_LBN), lambda g, i, j: (i, g, 0, 0))],
        out_shape=[_sds((nb, b, l // q, q * LANE), BF16), _sds((b, nb, SUBLANE, 2 * S5_LBN))],
        scratch_shapes=[pltpu.VMEM((SUBLANE, 2 * S5_LBN), F32)],
        compiler_params=_cparams(3),
        name="s5_prompt",
    )(*([proj_q] * q), t_in, w_st, w_out, tab, d_skip)
    return y.reshape(nb, b * l, LANE), h_fin


def _rope_tables(pos):
    half = ROT_DIM // 2
    inv_freq = jnp.exp(-(2.0 * jnp.arange(half, dtype=F32) / ROT_DIM) * math.log(ROPE_THETA))
    ang = pos.astype(F32)[:, None] * inv_freq[None, :]
    cos, sin = jnp.cos(ang), jnp.sin(ang)
    l = pos.shape[0]
    one = jnp.ones((l, HEAD_DIM - ROT_DIM), F32)
    zero = jnp.zeros((l, HEAD_DIM - ROT_DIM), F32)
    zh = jnp.zeros((l, half), F32)
    cos_h = jnp.concatenate([cos, cos, one], axis=1)
    sa_h = jnp.concatenate([-sin, zh, zero], axis=1)
    sb_h = jnp.concatenate([zh, sin, zero], axis=1)
    two = lambda t: jnp.concatenate([t, t], axis=1)
    return two(cos_h), two(sa_h), two(sb_h)


def _rope_flat(x, cos, sa, sb):
    w = x.shape[1]
    n = w // LANE
    tile = lambda t: jnp.concatenate([t] * n, axis=1) if n > 1 else t
    half = ROT_DIM // 2
    return x * tile(cos) + pltpu.roll(x, w - half, 1) * tile(sa) + pltpu.roll(x, half, 1) * tile(sb)


def _attn_prompt_body(q_ref, k_ref, v_ref, cos_ref, sa_ref, sb_ref, sink_ref, wo_ref, y_ref, kn_ref, vn_ref,
                      kprev, vprev):
    j = pl.program_id(1)
    nb = pl.num_programs(1)
    w = WINDOW
    kvw = KV_HEADS * HEAD_DIM

    @pl.when(j == 0)
    def _():
        kprev[...] = jnp.zeros(kprev.shape, F32)
        vprev[...] = jnp.zeros(vprev.shape, F32)

    cos, sa, sb = cos_ref[...], sa_ref[...], sb_ref[...]
    q = _rope_flat(q_ref[...].astype(F32), cos, sa, sb) * (HEAD_DIM ** -0.5)
    k = _rope_flat(k_ref[...].astype(F32), cos, sa, sb)
    v = v_ref[...].astype(F32)
    kctx = jnp.concatenate([kprev[...], k], axis=0).astype(BF16)
    vctx = jnp.concatenate([vprev[...], v], axis=0).astype(BF16)
    rows = Q_PER_KV * w
    ti = lax.broadcasted_iota(jnp.int32, (rows, 2 * w), 0) % w
    ci = lax.broadcasted_iota(jnp.int32, (rows, 2 * w), 1)
    valid = (ci >= ti) & (ci <= ti + w) & ((ci >= w) | (j > 0))
    ones = jnp.ones((2 * w, LANE), BF16)
    qb = q.astype(BF16)
    outs = []
    for kv in range(KV_HEADS):
        kj = kctx[:, kv * HEAD_DIM:(kv + 1) * HEAD_DIM]
        vj = vctx[:, kv * HEAD_DIM:(kv + 1) * HEAD_DIM]
        heads = range(kv * Q_PER_KV, (kv + 1) * Q_PER_KV)
        qs = jnp.concatenate([qb[:, h * HEAD_DIM:(h + 1) * HEAD_DIM] for h in heads], axis=0)
        sink = jnp.concatenate([jnp.broadcast_to(sink_ref[0:1, h:h + 1], (w, 1)) for h in heads], axis=0)
        s = lax.dot_general(qs, kj, (((1,), (1,)), ((), ())), preferred_element_type=F32)
        s = jnp.where(valid, s, -jnp.inf)
        m = jnp.maximum(jnp.max(s, axis=-1, keepdims=True), sink)
        p = jnp.exp(s - m).astype(BF16)
        den = jnp.dot(p, ones, preferred_element_type=F32)[:, 0:HEAD_DIM] + jnp.exp(sink - m)
        o = jnp.dot(p, vj, preferred_element_type=F32) / den
        outs.extend(o[i * w:(i + 1) * w] for i in range(Q_PER_KV))
    o = jnp.concatenate(outs, axis=1)
    y_ref[...] = jnp.dot(o.astype(BF16), wo_ref[...], preferred_element_type=F32).astype(y_ref.dtype)
    kprev[...] = k
    vprev[...] = v

    @pl.when(j == nb - 1)
    def _():
        kn_ref[...] = k
        vn_ref[...] = v


def _attn_prompt(proj, cos, sa, sb, sinks, attn_o):
    b, l, _ = proj.shape
    d = D_MODEL
    w = WINDOW
    kvw = KV_HEADS * HEAD_DIM
    const2 = lambda i, j: (0, 0)
    tmap = lambda i, j: (j, 0)
    return pl.pallas_call(
        _attn_prompt_body,
        grid=(b, l // w),
        in_specs=[pl.BlockSpec((None, w, d), lambda i, j: (i, j, P_Q // d)),
                  pl.BlockSpec((None, w, kvw), lambda i, j: (i, j, P_K // kvw)),
                  pl.BlockSpec((None, w, kvw), lambda i, j: (i, j, P_V // kvw)),
                  pl.BlockSpec((w, LANE), tmap), pl.BlockSpec((w, LANE), tmap), pl.BlockSpec((w, LANE), tmap),
                  pl.BlockSpec((1, LANE), const2),
                  pl.BlockSpec((d, d), const2)],
        out_specs=[pl.BlockSpec((None, w, d), lambda i, j: (i, j, 0)),
                   pl.BlockSpec((None, w, kvw), lambda i, j: (i, 0, 0)),
                   pl.BlockSpec((None, w, kvw), lambda i, j: (i, 0, 0))],
        out_shape=[_sds((b, l, d), BF16), _sds((b, w, kvw)), _sds((b, w, kvw))],
        scratch_shapes=[pltpu.VMEM((w, kvw), F32), pltpu.VMEM((w, kvw), F32)],
        compiler_params=_cparams(2),
        name="attn_prompt",
    )(proj, proj, proj, cos, sa, sb, sinks, attn_o)


SAMPLE_TB = 8


def _mamba_sample_body(x_ref, bc_ref, dt_ref, cst_ref, ssm_ref, cw_ref, cb_ref, dtb_ref, alog_ref, dskip_ref,
                       exp_ref, y_ref, cnew_ref, ssmo_ref):
    tb = SAMPLE_TB
    new = jnp.concatenate([x_ref[...], bc_ref[...]], axis=1)
    conv = (cb_ref[...] + cw_ref[0:1, :] * cst_ref[0] + cw_ref[1:2, :] * cst_ref[1] + cw_ref[2:3, :] * cst_ref[2]
            + cw_ref[3:4, :] * new)
    cnew_ref[0] = cst_ref[1]
    cnew_ref[1] = cst_ref[2]
    cnew_ref[2] = new
    xbc = _silu(conv)
    xs = xbc[:, 0:M_INNER]
    gn = M_GROUPS * M_STATE
    bm = xbc[:, M_INNER:M_INNER + gn]
    cm = xbc[:, M_INNER + gn:].astype(BF16)
    dt = _softplus(dt_ref[...] + dtb_ref[...])
    a = dt * (-jnp.exp(alog_ref[...]))
    expand = exp_ref[...]
    dtx = _dot3_right(dt, expand) * xs
    da_x = jnp.exp(_dot3_right(a, expand))
    nl = M_INNER // LANE
    slab = jnp.concatenate([dtx[:, j * LANE:(j + 1) * LANE] for j in range(nl)]
                           + [da_x[:, j * LANE:(j + 1) * LANE] for j in range(nl)], axis=0)
    tt = slab.T
    hpg = M_HEADS // M_GROUPS
    gw = hpg * M_HEADDIM
    rowid = lax.broadcasted_iota(jnp.int32, (tb, gw), 0)
    ys = []
    for g in range(M_GROUPS):
        yacc = jnp.zeros((tb, gw), F32)
        for t in range(tb):
            parts = []
            for hl in range(hpg):
                h = g * hpg + hl
                j, hh = divmod(h * M_HEADDIM, LANE)
                col = j * tb + t
                dcol = tt[hh:hh + M_HEADDIM, col:col + 1]
                acol = tt[hh:hh + M_HEADDIM, nl * tb + col:nl * tb + col + 1]
                hn = acol * ssm_ref[t, h] + dcol * bm[t:t + 1, g * M_STATE:(g + 1) * M_STATE]
                ssmo_ref[t, h] = hn
                parts.append(hn)
            hng = jnp.concatenate(parts, axis=0).astype(BF16)
            yg = lax.dot_general(cm[:, g * M_STATE:(g + 1) * M_STATE], hng, (((1,), (1,)), ((), ())),
                                 preferred_element_type=F32)
            yacc = jnp.where(rowid == t, yg, yacc)
        ys.append(yacc)
    y_ref[...] = jnp.concatenate(ys, axis=1) + xs * dskip_ref[...]


def _mamba_sample(proj, dtp, conv_t, ssm, conv_w, conv_b, dt_bias, a_log, d_skip, expand):
    s = proj.shape[0]
    d = D_MODEL
    tb = SAMPLE_TB
    const = lambda i: (0, 0)
    return pl.pallas_call(
        _mamba_sample_body,
        grid=(s // tb,),
        in_specs=[pl.BlockSpec((tb, d), lambda i: (i, S_X // d)),
                  pl.BlockSpec((tb, d), lambda i: (i, S_BC // d)),
                  pl.BlockSpec((tb, LANE), lambda i: (i, 0)),
                  pl.BlockSpec((CONV_W - 1, tb, CONV_DIM), lambda i: (0, i, 0)),
                  pl.BlockSpec((tb, M_HEADS, M_HEADDIM, M_STATE), lambda i: (i, 0, 0, 0)),
                  pl.BlockSpec((CONV_W, CONV_DIM), const), pl.BlockSpec((1, CONV_DIM), const),
                  pl.BlockSpec((1, LANE), const), pl.BlockSpec((1, LANE), const), pl.BlockSpec((1, d), const),
                  pl.BlockSpec((LANE, d), const)],
        out_specs=[pl.BlockSpec((tb, d), lambda i: (i, 0)),
                   pl.BlockSpec((CONV_W - 1, tb, CONV_DIM), lambda i: (0, i, 0)),
                   pl.BlockSpec((tb, M_HEADS, M_HEADDIM, M_STATE), lambda i: (i, 0, 0, 0))],
        out_shape=[_sds((s, d)), _sds((CONV_W - 1, s, CONV_DIM)), _sds((s, M_HEADS, M_HEADDIM, M_STATE))],
        compiler_params=_cparams(1),
        name="mamba_sample",
    )(proj, proj, dtp, conv_t, ssm, conv_w, conv_b, dt_bias, a_log, d_skip, expand)


def _gate_norm_proj_body(y_ref, z_ref, nw_ref, w_ref, o_ref):
    y = _group_norm_gate(y_ref[...], z_ref[...], nw_ref[...])
    o_ref[...] = jnp.dot(y.astype(BF16), w_ref[...], preferred_element_type=F32)


def _gate_norm_proj(y, proj, m_norm_w, m_proj):
    s, d = y.shape
    const = lambda i: (0, 0)
    return pl.pallas_call(
        _gate_norm_proj_body,
        grid=(1,),
        in_specs=[pl.BlockSpec((s, d), const), pl.BlockSpec((s, d), lambda i: (0, S_Z // d)),
                  pl.BlockSpec((1, d), const), pl.BlockSpec((d, d), const)],
        out_specs=pl.BlockSpec((s, d), const),
        out_shape=_sds((s, d)),
        compiler_params=_cparams(1),
        name="gate_norm_proj",
    )(y, proj, m_norm_w, m_proj)


def _s5_sample_body(u_ref, hre_ref, him_ref, wbr_ref, wbi_ref, wcr_ref, wci_ref, tab_ref, d_ref, y_ref, ore_ref,
                    oim_ref):
    u = u_ref[...]
    ub = u.astype(BF16)
    bre = jnp.dot(ub, wbr_ref[...], preferred_element_type=F32)
    bim = jnp.dot(ub, wbi_ref[...], preferred_element_type=F32)
    ar = tab_ref[0]
    ai = tab_ref[1]
    h0r = hre_ref[...]
    h0i = him_ref[...]
    hr = ar * h0r - ai * h0i + bre
    hi = ar * h0i + ai * h0r + bim
    ore_ref[...] = hr
    oim_ref[...] = hi
    y = (jnp.dot(hr.astype(BF16), wcr_ref[...], preferred_element_type=F32)
         + jnp.dot(hi.astype(BF16), wci_ref[...], preferred_element_type=F32) + d_ref[...] * u)
    y_ref[...] = _gelu_tanh(y)


def _s5_sample(proj, h_re, h_im, wbr, wbi, wcr, wci, tab, d_skip):
    s = proj.shape[0]
    nb = S5_GROUPS // S5_LB
    ub = S_U // LANE
    wmap = lambda g: (g, 0, 0)
    hspec = pl.BlockSpec((s, S5_LBN), lambda g: (0, g))
    return pl.pallas_call(
        _s5_sample_body,
        grid=(nb,),
        in_specs=[pl.BlockSpec((s, LANE), lambda g: (0, ub + g)), hspec, hspec,
                  pl.BlockSpec((None, LANE, S5_LBN), wmap), pl.BlockSpec((None, LANE, S5_LBN), wmap),
                  pl.BlockSpec((None, S5_LBN, LANE), wmap), pl.BlockSpec((None, S5_LBN, LANE), wmap),
                  pl.BlockSpec((None, 2, 1, S5_LBN), lambda g: (g, 0, 0, 0)),
                  pl.BlockSpec((1, LANE), lambda g: (0, g))],
        out_specs=[pl.BlockSpec((None, s, LANE), wmap), hspec, hspec],
        out_shape=[_sds((nb, s, LANE)), _sds(h_re.shape), _sds(h_im.shape)],
        compiler_params=_cparams(1),
        name="s5_sample",
    )(proj, h_re, h_im, wbr, wbi, wcr, wci, tab, d_skip)


def _rope_sample_body(q_ref, k_ref, cos_ref, sa_ref, sb_ref, qo_ref, ko_ref):
    cos, sa, sb = cos_ref[...], sa_ref[...], sb_ref[...]
    qo_ref[...] = _rope_flat(q_ref[...], cos, sa, sb) * (HEAD_DIM ** -0.5)
    ko_ref[...] = _rope_flat(k_ref[...], cos, sa, sb)


def _rope_sample(proj, cos, sa, sb):
    s = proj.shape[0]
    kvw = KV_HEADS * HEAD_DIM
    const = lambda i: (0, 0)
    return pl.pallas_call(
        _rope_sample_body,
        grid=(1,),
        in_specs=[pl.BlockSpec((s, QX), lambda i: (0, S_QX // QX)), pl.BlockSpec((s, kvw), lambda i: (0, S_K // kvw)),
                  pl.BlockSpec((1, LANE), const), pl.BlockSpec((1, LANE), const), pl.BlockSpec((1, LANE), const)],
        out_specs=[pl.BlockSpec((s, QX), const), pl.BlockSpec((s, kvw), const)],
        out_shape=[_sds((s, QX)), _sds((s, kvw))],
        compiler_params=_cparams(1),
        name="rope_sample",
    )(proj, proj, cos, sa, sb)


def _attn_sample_body(q_ref, kc_ref, vc_ref, kn_ref, vn_ref, sink_ref, o_ref, ko_ref, vo_ref):
    q = q_ref[...]
    kc = kc_ref[...]
    vc = vc_ref[...]
    kn = kn_ref[...]
    vn = vn_ref[...]
    w = kc.shape[1]
    s_c = lax.dot_general(q.astype(BF16), kc.astype(BF16), (((2,), (2,)), ((0,), (0,))),
                          preferred_element_type=F32)
    s_n = jnp.sum(q * kn, axis=-1, keepdims=True)
    sink = sink_ref[...][None]
    m = jnp.maximum(jnp.maximum(jnp.max(s_c, axis=-1, keepdims=True), s_n), sink)
    p_c = jnp.exp(s_c - m)
    p_n = jnp.exp(s_n - m)
    den = jnp.sum(p_c, axis=-1, keepdims=True) + p_n + jnp.exp(sink - m)
    o = lax.dot_general((p_c / den).astype(BF16), vc.astype(BF16), (((2,), (1,)), ((0,), (0,))),
                        preferred_element_type=F32)
    o_ref[...] = o + (p_n / den) * vn
    ko_ref[:, 0:w - 1, :] = kc[:, 1:w, :]
    ko_ref[:, w - 1:w, :] = kn
    vo_ref[:, 0:w - 1, :] = vc[:, 1:w, :]
    vo_ref[:, w - 1:w, :] = vn


def _attn_sample(qx, kc, vc, kn, proj3, sinks):
    s, w, kvw = kc.shape
    tb = SAMPLE_TB
    row3 = lambda i: (i, 0, 0)
    return pl.pallas_call(
        _attn_sample_body,
        grid=(s // tb,),
        in_specs=[pl.BlockSpec((tb, A_HEADS, kvw), row3), pl.BlockSpec((tb, w, kvw), row3),
                  pl.BlockSpec((tb, w, kvw), row3), pl.BlockSpec((tb, 1, kvw), row3),
                  pl.BlockSpec((tb, 1, kvw), lambda i: (i, 0, S_V // kvw)),
                  pl.BlockSpec((A_HEADS, 1), lambda i: (0, 0))],
        out_specs=[pl.BlockSpec((tb, A_HEADS, kvw), row3), pl.BlockSpec((tb, w, kvw), row3),
                   pl.BlockSpec((tb, w, kvw), row3)],
        out_shape=[_sds((s, A_HEADS, kvw)), _sds((s, w, kvw)), _sds((s, w, kvw))],
        compiler_params=_cparams(1),
        name="attn_sample",
    )(qx, kc, vc, kn, proj3, sinks)


def _dense_body(x_ref, w_ref, o_ref):
    o_ref[...] = jnp.dot(x_ref[...].astype(BF16), w_ref[...], preferred_element_type=F32)


def _dense(x, w, tn):
    m, k = x.shape
    n = w.shape[1]
    return pl.pallas_call(
        _dense_body,
        grid=(n // tn,),
        in_specs=[pl.BlockSpec((m, k), lambda j: (0, 0)), pl.BlockSpec((k, tn), lambda j: (0, j))],
        out_specs=pl.BlockSpec((m, tn), lambda j: (0, j)),
        out_shape=_sds((m, n)),
        compiler_params=_cparams(1),
        name="dense",
    )(x, w)


def _pad_lanes(v):
    return jnp.zeros((1, LANE), F32).at[0, :v.shape[0]].set(v.astype(F32))


def _layer_weights(l, w_in, attn_o):
    w = w_in[l]
    seg = lambda a, b: w[:, a:b]
    z, xbc, dt, u = seg(OFF_Z, OFF_XBC), seg(OFF_XBC, OFF_DT), seg(OFF_DT, OFF_U), seg(OFF_U, OFF_Q)
    q, k, v, g = seg(OFF_Q, OFF_K), seg(OFF_K, OFF_V), seg(OFF_V, OFF_G), seg(OFF_G, IN_COLS)
    w_prompt = jnp.concatenate([z, xbc, u, q, g, k, v], axis=1).astype(BF16)
    sel = (jnp.arange(A_HEADS)[:, None] // Q_PER_KV == jnp.arange(KV_HEADS)[None, :]).astype(F32)
    qx = (q.reshape(D_MODEL, A_HEADS, 1, HEAD_DIM) * sel[None, :, :, None]).reshape(D_MODEL, QX)
    w_sample = jnp.concatenate([z, xbc, u, qx, g, k, v], axis=1).astype(BF16)
    w_dt = jnp.zeros((D_MODEL, LANE), F32).at[:, :M_HEADS].set(dt).astype(BF16)
    ao = attn_o[l].reshape(A_HEADS, 1, HEAD_DIM, D_MODEL) * sel[:, :, None, None]
    return w_prompt, w_sample, w_dt, ao.reshape(QX, D_MODEL).astype(BF16)


def kernel(x_prompt, x_sample, state_ssm, state_conv, state_s5_re, state_s5_im, cache_k, cache_v, norm1_w, w_in,
           conv_w, conv_b, dt_bias, a_log, m_d, m_norm_w, m_proj, s5_lam_re, s5_lam_im, s5_log_step, s5_b_re,
           s5_b_im, s5_c_re, s5_c_im, s5_d, s5_glu_w, attn_sinks, attn_o, w_out, norm2_w, mlp_up, mlp_down,
           final_norm_w):
    b, l, d = x_prompt.shape
    s = x_sample.shape[0]
    kvw = KV_HEADS * HEAD_DIM
    nb = S5_GROUPS // S5_LB
    xp = x_prompt.reshape(b * l, d)
    xs = x_sample.reshape(s, d)
    cos_p, sa_p, sb_p = _rope_tables(jnp.arange(l, dtype=jnp.int32))
    cos_s, sa_s, sb_s = _rope_tables(jnp.full((1,), PAST_LEN, jnp.int32))
    expand = (jnp.arange(LANE)[:, None] == jnp.arange(M_INNER)[None, :] // M_HEADDIM).astype(BF16)
    fnw = final_norm_w[None]
    outs = [[] for _ in range(12)]
    for li in range(DEPTH):
        final = li == DEPTH - 1
        w_prompt, w_sample, w_dt, ao_x = _layer_weights(li, w_in, attn_o)
        n1 = norm1_w[li][None]
        cw, cb = conv_w[li], conv_b[li][None]
        dtb, alog = _pad_lanes(dt_bias[li]), _pad_lanes(a_log[li])
        dsk = jnp.repeat(m_d[li], M_HEADDIM)[None]
        mnw = m_norm_w[li][None]
        mpj = m_proj[li].astype(BF16)
        s5_chunked, s5_step = _s5_operators(s5_lam_re[li], s5_lam_im[li], s5_log_step[li], s5_b_re[li], s5_b_im[li],
                                            s5_c_re[li], s5_c_im[li])
        s5d = s5_d[li][None]
        sinks = attn_sinks[li]
        ao = attn_o[li].astype(BF16)
        mlp_w = (s5_glu_w[li].astype(BF16), w_out[li].astype(BF16), norm2_w[li][None], mlp_up[li].astype(BF16),
                 mlp_down[li].astype(BF16), fnw)

        proj, dtp = _norm_proj(xp, n1, w_prompt, w_dt, 2048, 512, BF16)
        proj3 = proj.reshape(b, l, P_COLS)
        ym, conv_p, ssm_p = _mamba_prompt(proj3, dtp.reshape(b, l, LANE), cw, cb, dtb, alog, dsk, mnw, expand, mpj)
        ys, h_p = _s5_prompt(proj3, *s5_chunked, s5d, 256)
        hre_p, him_p = h_p[:, :, 0, :S5_LBN], h_p[:, :, 0, S5_LBN:]
        ya, k_p, v_p = _attn_prompt(proj3, cos_p, sa_p, sb_p, _pad_lanes(sinks), ao)
        xp = _merge_mlp(xp, proj, P_G, ym.reshape(b * l, d), ys, ya.reshape(b * l, d), *mlp_w, 512, final)

        sproj, sdt = _norm_proj(xs, n1, w_sample, w_dt, s, 512, F32)
        y_pre, conv_t, ssm_s = _mamba_sample(sproj, sdt, state_conv[li].transpose(1, 0, 2), state_ssm[li], cw, cb,
                                             dtb, alog, dsk, expand)
        sym = _gate_norm_proj(y_pre, sproj, mnw, mpj)
        sys_, hre_s, him_s = _s5_sample(sproj, state_s5_re[li].reshape(s, S5_GROUPS * S5_STATE),
                                        state_s5_im[li].reshape(s, S5_GROUPS * S5_STATE), *s5_step, s5d)
        qx_rot, k_rot = _rope_sample(sproj, cos_s, sa_s, sb_s)
        o, k_s, v_s = _attn_sample(qx_rot.reshape(s, A_HEADS, kvw), cache_k[li].reshape(s, WINDOW, kvw),
                                   cache_v[li].reshape(s, WINDOW, kvw), k_rot.reshape(s, 1, kvw),
                                   sproj.reshape(s, 1, S_COLS), sinks[:, None])
        sya = _dense(o.reshape(s, QX), ao_x, 512)
        xs = _merge_mlp(xs, sproj, S_G, sym, sys_, sya, *mlp_w, s, final)

        for i, val in enumerate((
                ssm_p, ssm_s, conv_p, conv_t.transpose(1, 0, 2),
                hre_p.reshape(b, S5_GROUPS, S5_STATE), hre_s.reshape(s, S5_GROUPS, S5_STATE),
                him_p.reshape(b, S5_GROUPS, S5_STATE), him_s.reshape(s, S5_GROUPS, S5_STATE),
                k_p.reshape(b, WINDOW, KV_HEADS, HEAD_DIM), k_s.reshape(s, WINDOW, KV_HEADS, HEAD_DIM),
                v_p.reshape(b, WINDOW, KV_HEADS, HEAD_DIM), v_s.reshape(s, WINDOW, KV_HEADS, HEAD_DIM))):
            outs[i].append(val)
    return (xp.reshape(b, l, d), xs.reshape(s, 1, d)) + tuple(jnp.stack(o) for o in outs)
```

```python
import functools
import math

import jax
import jax.numpy as jnp
from jax import lax
from jax.experimental import pallas as pl
from jax.experimental.pallas import tpu as pltpu

F32 = jnp.float32
BF16 = jnp.bfloat16

D_MODEL = 1024
DEPTH = 4
PAST_LEN = 8192
M_HEADDIM = 64
M_HEADS = 16
M_INNER = 1024
M_GROUPS = 4
M_STATE = 128
CONV_W = 4
CONV_DIM = 2048
CHUNK = 128
S5_GROUPS = 64
S5_GSIZE = 16
S5_STATE = 64
HEAD_DIM = 64
A_HEADS = 16
KV_HEADS = 4
Q_PER_KV = 4
ROT_DIM = 16
ROPE_THETA = 500000.0
WINDOW = 128
D_FF = 4096
EPS = 1e-6

OFF_Z, OFF_XBC, OFF_DT, OFF_U, OFF_Q, OFF_K, OFF_V, OFF_G, IN_COLS = 0, 1024, 3072, 3088, 4112, 5136, 5392, 5648, 8720

LANE = 128
SUBLANE = 8
S5_LB = 8
S5_LBN = S5_LB * S5_STATE
VMEM_LIMIT = 56 * 1024 * 1024

P_Z, P_X, P_BC, P_U, P_Q, P_G, P_K, P_V, P_COLS = 0, 1024, 2048, 3072, 4096, 5120, 8192, 8448, 8704
S_Z, S_X, S_BC, S_U, S_QX, S_G, S_K, S_V, S_COLS = 0, 1024, 2048, 3072, 4096, 8192, 11264, 11520, 11776
QX = A_HEADS * KV_HEADS * HEAD_DIM


def _sds(shape, dtype=F32):
    return jax.ShapeDtypeStruct(shape, dtype)


def _cparams(n_axes):
    return pltpu.CompilerParams(dimension_semantics=("arbitrary",) * n_axes, vmem_limit_bytes=VMEM_LIMIT)


def _sigmoid(x):
    return 1.0 / (1.0 + jnp.exp(-x))


def _silu(x):
    return x * _sigmoid(x)


def _softplus(x):
    return jnp.maximum(x, 0.0) + jnp.log(1.0 + jnp.exp(-jnp.abs(x)))


def _gelu_tanh(x):
    return 0.5 * x * (1.0 + jnp.tanh(math.sqrt(2.0 / math.pi) * (x + 0.044715 * (x * x * x))))


def _bdot(a, b):
    return jnp.dot(a.astype(BF16), b.astype(BF16), preferred_element_type=F32)


def _split3(a):
    hi = a.astype(BF16)
    r = a - hi.astype(F32)
    mid = r.astype(BF16)
    lo = (r - mid.astype(F32)).astype(BF16)
    return hi, mid, lo


def _dot3_right(a, sel):
    hi, mid, lo = _split3(a)
    return (jnp.dot(hi, sel, preferred_element_type=F32) + jnp.dot(mid, sel, preferred_element_type=F32)
            + jnp.dot(lo, sel, preferred_element_type=F32))


def _dot3_left(sel, a):
    hi, mid, lo = _split3(a)
    return (jnp.dot(sel, hi, preferred_element_type=F32) + jnp.dot(sel, mid, preferred_element_type=F32)
            + jnp.dot(sel, lo, preferred_element_type=F32))


def _rms(x, w):
    return x * lax.rsqrt(jnp.mean(x * x, axis=-1, keepdims=True) + EPS) * w


def _norm_proj_body(x_ref, nw_ref, w_ref, wdt_ref, o_ref, odt_ref, h_scr):
    @pl.when(pl.program_id(1) == 0)
    def _():
        hb = _rms(x_ref[...], nw_ref[...]).astype(BF16)
        h_scr[...] = hb
        odt_ref[...] = jnp.dot(hb, wdt_ref[...], preferred_element_type=F32)

    o_ref[...] = jnp.dot(h_scr[...], w_ref[...], preferred_element_type=F32).astype(o_ref.dtype)


def _norm_proj(x, nw, w, wdt, tm, tn, out_dtype):
    m, d = x.shape
    n = w.shape[1]
    return pl.pallas_call(
        _norm_proj_body,
        grid=(m // tm, n // tn),
        in_specs=[pl.BlockSpec((tm, d), lambda i, j: (i, 0)),
                  pl.BlockSpec((1, d), lambda i, j: (0, 0)),
                  pl.BlockSpec((d, tn), lambda i, j: (0, j)),
                  pl.BlockSpec((d, LANE), lambda i, j: (0, 0))],
        out_specs=[pl.BlockSpec((tm, tn), lambda i, j: (i, j)),
                   pl.BlockSpec((tm, LANE), lambda i, j: (i, 0))],
        out_shape=[_sds((m, n), out_dtype), _sds((m, LANE))],
        scratch_shapes=[pltpu.VMEM((tm, d), BF16)],
        compiler_params=_cparams(2),
        name="norm_proj",
    )(x, nw, w, wdt)


def _merge_mlp_body(final, x_ref, g0_ref, g1_ref, g2_ref, ym_ref, ys_ref, ya_ref, glu_ref, wo_ref, n2_ref, up_ref,
                    dn_ref, fn_ref, o_ref):
    ys = jnp.concatenate([ys_ref[g] for g in range(ys_ref.shape[0])], axis=1)
    glu = jnp.dot(ys.astype(BF16), glu_ref[...], preferred_element_type=F32)
    y_s = glu[:, :D_MODEL] * _sigmoid(glu[:, D_MODEL:])
    f32 = lambda r: r[...].astype(F32)
    merged = _sigmoid(f32(g0_ref)) * f32(ym_ref) + _sigmoid(f32(g1_ref)) * y_s + _sigmoid(f32(g2_ref)) * f32(ya_ref)
    x1 = x_ref[...] + jnp.dot(merged.astype(BF16), wo_ref[...], preferred_element_type=F32)
    h2 = _rms(x1, n2_ref[...]).astype(BF16)
    acc = x1
    fc = 1024
    for c in range(D_FF // fc):
        a = jnp.dot(h2, up_ref[:, c * fc:(c + 1) * fc], preferred_element_type=F32)
        a = jnp.square(jnp.maximum(a, 0.0))
        acc = acc + jnp.dot(a.astype(BF16), dn_ref[c * fc:(c + 1) * fc, :], preferred_element_type=F32)
    o_ref[...] = _rms(acc, fn_ref[...]) if final else acc


def _merge_mlp(x, proj, g_col, ym, ys, ya, glu_w, w_out, n2, up, dn, fnw, tm, final):
    m, d = x.shape
    gb = g_col // d
    nlb = d // LANE
    row = lambda i: (i, 0)
    const = lambda i: (0, 0)
    wspec = lambda shape: pl.BlockSpec(shape, const, pipeline_mode=pl.Buffered(1))
    return pl.pallas_call(
        functools.partial(_merge_mlp_body, final),
        grid=(m // tm,),
        in_specs=[pl.BlockSpec((tm, d), row),
                  pl.BlockSpec((tm, d), lambda i: (i, gb)),
                  pl.BlockSpec((tm, d), lambda i: (i, gb + 1)),
                  pl.BlockSpec((tm, d), lambda i: (i, gb + 2)),
                  pl.BlockSpec((tm, d), row), pl.BlockSpec((nlb, tm, LANE), lambda i: (0, i, 0)),
                  pl.BlockSpec((tm, d), row),
                  wspec((d, 2 * d)), wspec((d, d)), wspec((1, d)), wspec((d, D_FF)), wspec((D_FF, d)),
                  wspec((1, d))],
        out_specs=pl.BlockSpec((tm, d), row),
        out_shape=_sds((m, d)),
        compiler_params=_cparams(1),
        name="merge_mlp",
    )(x, proj, proj, proj, ym, ys, ya, glu_w, w_out, n2, up, dn, fnw)


def _group_norm_gate(y, z, nw):
    y = y * _silu(z)
    gw = M_INNER // M_GROUPS
    parts = []
    for g in range(M_GROUPS):
        yg = y[:, g * gw:(g + 1) * gw]
        parts.append(yg * lax.rsqrt(jnp.mean(yg * yg, axis=-1, keepdims=True) + EPS))
    return jnp.concatenate(parts, axis=1) * nw


def _mamba_prompt_body(z_ref, x_ref, bc_ref, dt_ref, cw_ref, cb_ref, dtb_ref, alog_ref, dskip_ref, nw_ref,
                       exp_ref, mproj_ref, y_ref, conv_ref, ssm_ref, xprev, ht):
    c = pl.program_id(1)
    nc = pl.num_programs(1)
    q = CHUNK
    tail = xprev.shape[0]

    @pl.when(c == 0)
    def _():
        xprev[...] = jnp.zeros(xprev.shape, xprev.dtype)
        ht[...] = jnp.zeros(ht.shape, F32)

    xcur = jnp.concatenate([x_ref[...], bc_ref[...]], axis=1)
    xext = jnp.concatenate([xprev[...], xcur], axis=0)
    ri = lax.broadcasted_iota(jnp.int32, (q, tail + q), 0)
    ci = lax.broadcasted_iota(jnp.int32, (q, tail + q), 1)
    conv = cb_ref[...] + cw_ref[CONV_W - 1:CONV_W, :] * xcur.astype(F32)
    for k in range(CONV_W - 1):
        shift = (ci == ri + (tail - (CONV_W - 1) + k)).astype(BF16)
        conv = conv + cw_ref[k:k + 1, :] * jnp.dot(shift, xext, preferred_element_type=F32)
    last_rows = xcur[q - tail:q, :]
    xprev[...] = last_rows

    @pl.when(c == nc - 1)
    def _():
        conv_ref[...] = last_rows.astype(F32)[tail - (CONV_W - 1):tail, :]

    xbc = _silu(conv)
    xs = xbc[:, 0:M_INNER]
    gn = M_GROUPS * M_STATE
    bm = xbc[:, M_INNER:M_INNER + gn]
    cm = xbc[:, M_INNER + gn:]

    dt = _softplus(dt_ref[...] + dtb_ref[...])
    a = dt * (-jnp.exp(alog_ref[...]))
    ri = lax.broadcasted_iota(jnp.int32, (q, q), 0)
    ci = lax.broadcasted_iota(jnp.int32, (q, q), 1)
    causal = ci <= ri
    tri = causal.astype(BF16)
    acum = _dot3_left(tri, a)
    expand = exp_ref[...]
    acum_x = _dot3_right(acum, expand)
    dt_x = _dot3_right(dt, expand)
    acum_last = acum_x[q - 1:q, :]
    xdt = xs * dt_x
    xw = xs * (jnp.exp(acum_last - acum_x) * dt_x)
    e_acum = jnp.exp(acum_x)
    chunk_decay = jnp.exp(acum_last)
    acum_t = acum.T

    hpg = M_HEADS // M_GROUPS
    gw = hpg * M_HEADDIM
    y_parts = []
    for g in range(M_GROUPS):
        cg = cm[:, g * M_STATE:(g + 1) * M_STATE].astype(BF16)
        bg = bm[:, g * M_STATE:(g + 1) * M_STATE]
        bgb = bg.astype(BF16)
        cb = lax.dot_general(cg, bgb, (((1,), (1,)), ((), ())), preferred_element_type=F32)
        h_prev = ht[g]
        y_off = jnp.dot(cg, h_prev.astype(BF16), preferred_element_type=F32) * e_acum[:, g * gw:(g + 1) * gw]
        yd = []
        for hl in range(hpg):
            h = g * hpg + hl
            seg = acum[:, h:h + 1] - acum_t[h:h + 1, :]
            lmat = jnp.exp(jnp.where(causal, seg, -jnp.inf))
            mh = (cb * lmat).astype(BF16)
            yd.append(jnp.dot(mh, xdt[:, h * M_HEADDIM:(h + 1) * M_HEADDIM].astype(BF16),
                              preferred_element_type=F32))
        y_parts.append(jnp.concatenate(yd, axis=1) + y_off)
        s_new = jnp.dot(bg.T.astype(BF16), xw[:, g * gw:(g + 1) * gw].astype(BF16), preferred_element_type=F32)
        ht[g] = h_prev * chunk_decay[:, g * gw:(g + 1) * gw] + s_new
    y = jnp.concatenate(y_parts, axis=1) + xs * dskip_ref[...]
    y = _group_norm_gate(y, z_ref[...].astype(F32), nw_ref[...])
    y_ref[...] = jnp.dot(y.astype(BF16), mproj_ref[...], preferred_element_type=F32).astype(y_ref.dtype)

    @pl.when(c == nc - 1)
    def _():
        for g in range(M_GROUPS):
            ssm_ref[g * hpg:(g + 1) * hpg] = ht[g].T.reshape(hpg, M_HEADDIM, M_STATE)


def _mamba_prompt(proj, dtp, conv_w, conv_b, dt_bias, a_log, d_skip, m_norm_w, expand, m_proj):
    b, l, _ = proj.shape
    d = D_MODEL
    nc = l // CHUNK
    const2 = lambda i, j: (0, 0)
    return pl.pallas_call(
        _mamba_prompt_body,
        grid=(b, nc),
        in_specs=[pl.BlockSpec((None, CHUNK, d), lambda i, j: (i, j, P_Z // d)),
                  pl.BlockSpec((None, CHUNK, d), lambda i, j: (i, j, P_X // d)),
                  pl.BlockSpec((None, CHUNK, d), lambda i, j: (i, j, P_BC // d)),
                  pl.BlockSpec((None, CHUNK, LANE), lambda i, j: (i, j, 0)),
                  pl.BlockSpec((CONV_W, CONV_DIM), const2),
                  pl.BlockSpec((1, CONV_DIM), const2),
                  pl.BlockSpec((1, LANE), const2),
                  pl.BlockSpec((1, LANE), const2),
                  pl.BlockSpec((1, d), const2),
                  pl.BlockSpec((1, d), const2),
                  pl.BlockSpec((LANE, d), const2),
                  pl.BlockSpec((d, d), const2)],
        out_specs=[pl.BlockSpec((None, CHUNK, d), lambda i, j: (i, j, 0)),
                   pl.BlockSpec((None, CONV_W - 1, CONV_DIM), lambda i, j: (i, 0, 0)),
                   pl.BlockSpec((None, M_HEADS, M_HEADDIM, M_STATE), lambda i, j: (i, 0, 0, 0))],
        out_shape=[_sds((b, l, d), BF16), _sds((b, CONV_W - 1, CONV_DIM)), _sds((b, M_HEADS, M_HEADDIM, M_STATE))],
        scratch_shapes=[pltpu.VMEM((2 * SUBLANE, CONV_DIM), BF16),
                        pltpu.VMEM((M_GROUPS, M_STATE, (M_HEADS // M_GROUPS) * M_HEADDIM), F32)],
        compiler_params=_cparams(2),
        name="mamba_prompt",
    )(proj, proj, proj, dtp, conv_w, conv_b, dt_bias, a_log, d_skip, m_norm_w, expand, m_proj)


S5_Q = 8


def _cmul(a, b):
    return a[0] * b[0] - a[1] * b[1], a[0] * b[1] + a[1] * b[0]


def _s5_disc_body(lr_ref, li_ref, ls_ref, bre_ref, bim_ref, cre_ref, cim_ref, p1re_ref, p1im_ref, pqre_ref, pqim_ref,
                  bare_ref, baim_ref, care_ref, caim_ref):
    lr = lr_ref[...]
    li = li_ref[...]
    step = jnp.exp(ls_ref[...])
    mag = jnp.exp(lr * step)
    ab = (mag * jnp.cos(li * step), mag * jnp.sin(li * step))
    den = lr * lr + li * li
    nr = ab[0] - 1.0
    ni = ab[1]
    f = ((nr * lr + ni * li) / den, (ni * lr - nr * li) / den)
    p1re_ref[...] = ab[0]
    p1im_ref[...] = ab[1]
    pows = [(jnp.ones_like(lr), jnp.zeros_like(lr))]
    for _ in range(S5_Q):
        pows.append(_cmul(pows[-1], ab))
    r = pows[S5_Q]
    for k in range(SUBLANE):
        pqre_ref[k] = r[0]
        pqim_ref[k] = r[1]
        r = _cmul(r, pows[S5_Q])
    for i in range(S5_GSIZE):
        bb = _cmul(f, (bre_ref[i], bim_ref[i]))
        for s in range(S5_Q):
            v = _cmul(pows[S5_Q - 1 - s], bb)
            bare_ref[s, i] = v[0]
            baim_ref[s, i] = v[1]
        c = (cre_ref[i], cim_ref[i])
        for k in range(S5_Q + 1):
            v = _cmul(c, pows[k])
            care_ref[k, i] = v[0]
            caim_ref[k, i] = v[1]


def _s5_disc(lam_re, lam_im, log_step, b_re_t, b_im_t, c_re_t, c_im_t):
    g, n = lam_re.shape
    gn = _sds((g, n))
    pq = _sds((SUBLANE, g, n))
    ba = _sds((S5_Q, S5_GSIZE, g, n))
    ca = _sds((S5_Q + 1, S5_GSIZE, g, n))
    return pl.pallas_call(
        _s5_disc_body,
        out_shape=[gn, gn, pq, pq, ba, ba, ca, ca],
        name="s5_disc",
    )(lam_re, lam_im, log_step, b_re_t, b_im_t, c_re_t, c_im_t)


def _hi_lo(a):
    hi = a.astype(BF16)
    return hi, (a - hi.astype(F32)).astype(BF16)


def _s5_kmat_body(car_ref, cai_ref, bbr_ref, bbi_ref, k_ref):
    dn = (((2,), (2,)), ((0,), (0,)))

    def mm(a, b):
        ah, al = _hi_lo(a)
        bh, bl = _hi_lo(b)
        d = lambda x, y: lax.dot_general(x, y, dn, preferred_element_type=F32)
        return d(ah, bh) + d(ah, bl) + d(al, bh)

    k_ref[...] = mm(car_ref[...], bbr_ref[...]) - mm(cai_ref[...], bbi_ref[...])


def _s5_kmat(ca_re, ca_im, bb_re, bb_im):
    g, m, _ = ca_re.shape
    return pl.pallas_call(
        _s5_kmat_body,
        out_shape=_sds((g, m, S5_GSIZE)),
        name="s5_kmat",
    )(ca_re, ca_im, bb_re, bb_im)


def _s5_operators(lam_re, lam_im, log_step, b_re, b_im, c_re, c_im):
    nb, q = S5_GROUPS // S5_LB, S5_Q
    p1_re, p1_im, pq_re, pq_im, ba_re, ba_im, ca_re, ca_im = _s5_disc(
        lam_re, lam_im, log_step[:, None], b_re.transpose(2, 0, 1), b_im.transpose(2, 0, 1),
        c_re.transpose(1, 0, 2), c_im.transpose(1, 0, 2))
    rows_k = lambda ca: ca[:q].transpose(2, 0, 1, 3).reshape(S5_GROUPS, q * S5_GSIZE, S5_STATE)
    kmat = _s5_kmat(rows_k(ca_re), rows_k(ca_im), ba_re[q - 1].transpose(1, 0, 2), ba_im[q - 1].transpose(1, 0, 2))
    eye = jnp.eye(S5_LB, dtype=F32)

    k5 = kmat.reshape(nb, S5_LB, q, S5_GSIZE, S5_GSIZE)
    lag = jnp.arange(q)[None, :] - jnp.arange(q)[:, None]
    kk = jnp.where((lag >= 0)[None, None, :, :, None, None], k5[:, :, jnp.clip(lag, 0, q - 1)], 0.0)
    t_in = jnp.einsum('bgstoi,gh->bsgitho', kk, eye).reshape(nb, q * LANE, q * LANE).astype(BF16)
    ba = jnp.stack([ba_re, ba_im], 0).reshape(2, q, S5_GSIZE, nb, S5_LB, S5_STATE)
    w_st = jnp.einsum('csibgn,gh->bsgichn', ba, eye).reshape(nb, q * LANE, 2 * S5_LBN).astype(BF16)
    ca = jnp.stack([ca_re[1:], -ca_im[1:]], 0).reshape(2, q, S5_GSIZE, nb, S5_LB, S5_STATE)
    w_out = jnp.einsum('ctobgn,gh->bcgntho', ca, eye).reshape(nb, 2 * S5_LBN, q * LANE).astype(BF16)

    lanes = lambda p: p.reshape(-1, nb, S5_LBN).transpose(1, 0, 2)
    rows = jnp.arange(SUBLANE)[None, :, None]
    pr, pi = lanes(pq_re), lanes(pq_im)
    tabs = [pr, pi]
    for d in (1, 2, 4):
        for p in (pr, pi):
            tabs.append(jnp.where(rows >= d, p[:, d - 1:d, :], 0.0))
    tab = jnp.stack(tabs, axis=1)

    def bd_in(bb):
        t = bb.transpose(1, 0, 2).reshape(nb, S5_LB, S5_GSIZE, S5_STATE)
        return jnp.einsum('bgin,gh->bgihn', t, eye).reshape(nb, LANE, S5_LBN).astype(BF16)

    def bd_out(cc):
        t = cc.reshape(nb, S5_LB, S5_GSIZE, S5_STATE)
        return jnp.einsum('bgon,gh->bgnho', t, eye).reshape(nb, S5_LBN, LANE).astype(BF16)

    step1 = (bd_in(ba_re[q - 1]), bd_in(ba_im[q - 1]), bd_out(c_re), bd_out(-c_im),
             jnp.stack([lanes(p1_re), lanes(p1_im)], axis=1))
    return (t_in, w_st, w_out, tab), step1


def _s5_scan_tile(xr, xi, tab_ref, car_re, car_im):
    for d, k in ((1, 2), (2, 4), (4, 6)):
        ar = tab_ref[k]
        ai = tab_ref[k + 1]
        sr = pltpu.roll(xr, d, 0)
        si = pltpu.roll(xi, d, 0)
        xr, xi = xr + ar * sr - ai * si, xi + ar * si + ai * sr
    pr = tab_ref[0]
    pi = tab_ref[1]
    hr = xr + pr * car_re - pi * car_im
    hi = xi + pr * car_im + pi * car_re
    return hr, hi


def _s5_prompt_body(u_ref, tin_ref, wst_ref, wout_ref, tab_ref, d_ref, y_ref, hfin_ref, car, uf, yf):
    q = S5_Q
    j = pl.program_id(2)
    nj = pl.num_programs(2)
    nrow = u_ref.shape[0] // q

    @pl.when(j == 0)
    def _():
        car[...] = jnp.zeros(car.shape, F32)

    uf[...] = u_ref[...].astype(F32)
    xb = jnp.concatenate([uf[pl.ds(t, nrow, stride=q), :] for t in range(q)], axis=1).astype(BF16)
    s = jnp.dot(xb, wst_ref[...], preferred_element_type=F32)
    car_re = car[:, 0:S5_LBN]
    car_im = car[:, S5_LBN:]
    first = lax.broadcasted_iota(jnp.int32, (SUBLANE, S5_LBN), 0) == 0
    hp = []
    for t in range(nrow // SUBLANE):
        rs = slice(t * SUBLANE, (t + 1) * SUBLANE)
        hr, hi = _s5_scan_tile(s[rs, 0:S5_LBN], s[rs, S5_LBN:], tab_ref, car_re, car_im)
        hp.append(jnp.concatenate([jnp.where(first, car_re, pltpu.roll(hr, 1, 0)),
                                   jnp.where(first, car_im, pltpu.roll(hi, 1, 0))], axis=1))
        car_re = jnp.broadcast_to(hr[SUBLANE - 1:SUBLANE, :], hr.shape)
        car_im = jnp.broadcast_to(hi[SUBLANE - 1:SUBLANE, :], hi.shape)
    car[:, 0:S5_LBN] = car_re
    car[:, S5_LBN:] = car_im
    h_in = jnp.concatenate(hp, axis=0).astype(BF16)
    d_row = jnp.concatenate([d_ref[...]] * q, axis=1)
    y = (jnp.dot(xb, tin_ref[...], preferred_element_type=F32)
         + jnp.dot(h_in, wout_ref[...], preferred_element_type=F32) + d_row * xb.astype(F32))
    y = _gelu_tanh(y)
    for t in range(q):
        yf[pl.ds(t, nrow, stride=q), :] = y[:, t * LANE:(t + 1) * LANE]
    y_ref[...] = yf[...].astype(y_ref.dtype)

    @pl.when(j == nj - 1)
    def _():
        hfin_ref[...] = car[...]


def _s5_prompt(proj, t_in, w_st, w_out, tab, d_skip, ntok):
    b, l, cols = proj.shape
    q = S5_Q
    nb = S5_GROUPS // S5_LB
    ub = P_U // LANE
    wmap = lambda g, i, j: (g, 0, 0)
    return pl.pallas_call(
        _s5_prompt_body,
        grid=(nb, b, l // ntok),
        in_specs=[pl.BlockSpec((None, ntok, LANE), lambda g, i, j: (i, j, ub + g)),
                  pl.BlockSpec((None, q * LANE, q * LANE), wmap),
                  pl.BlockSpec((None, q * LANE, 2 * S5_LBN), wmap),
                  pl.BlockSpec((None, 2 * S5_LBN, q * LANE), wmap),
                  pl.BlockSpec((None, 8, SUBLANE, S5_LBN), lambda g, i, j: (g, 0, 0, 0)),
                  pl.BlockSpec((1, LANE), lambda g, i, j: (0, g))],
        out_specs=[pl.BlockSpec((None, None, ntok, LANE), lambda g, i, j: (g, i, j, 0)),
                   pl.BlockSpec((None, None, SUBLANE, 2 * S5_LBN), lambda g, i, j: (i, g, 0, 0))],
        out_shape=[_sds((nb, b, l, LANE), BF16), _sds((b, nb, SUBLANE, 2 * S5_LBN))],
        scratch_shapes=[pltpu.VMEM((SUBLANE, 2 * S5_LBN), F32), pltpu.VMEM((ntok, LANE), F32),
                        pltpu.VMEM((ntok, LANE), F32)],
        compiler_params=_cparams(3),
        name="s5_prompt",
    )(proj, t_in, w_st, w_out, tab, d_skip)


def _rope_tables(pos):
    half = ROT_DIM // 2
    inv_freq = jnp.exp(-(2.0 * jnp.arange(half, dtype=F32) / ROT_DIM) * math.log(ROPE_THETA))
    ang = pos.astype(F32)[:, None] * inv_freq[None, :]
    cos, sin = jnp.cos(ang), jnp.sin(ang)
    l = pos.shape[0]
    one = jnp.ones((l, HEAD_DIM - ROT_DIM), F32)
    zero = jnp.zeros((l, HEAD_DIM - ROT_DIM), F32)
    zh = jnp.zeros((l, half), F32)
    cos_h = jnp.concatenate([cos, cos, one], axis=1)
    sa_h = jnp.concatenate([-sin, zh, zero], axis=1)
    sb_h = jnp.concatenate([zh, sin, zero], axis=1)
    two = lambda t: jnp.concatenate([t, t], axis=1)
    return two(cos_h), two(sa_h), two(sb_h)


def _rope_flat(x, cos, sa, sb):
    w = x.shape[1]
    n = w // LANE
    tile = lambda t: jnp.concatenate([t] * n, axis=1) if n > 1 else t
    half = ROT_DIM // 2
    return x * tile(cos) + pltpu.roll(x, w - half, 1) * tile(sa) + pltpu.roll(x, half, 1) * tile(sb)


def _attn_prompt_body(q_ref, k_ref, v_ref, cos_ref, sa_ref, sb_ref, sink_ref, wo_ref, y_ref, kn_ref, vn_ref,
                      kprev, vprev):
    j = pl.program_id(1)
    nb = pl.num_programs(1)
    w = WINDOW
    kvw = KV_HEADS * HEAD_DIM

    @pl.when(j == 0)
    def _():
        kprev[...] = jnp.zeros(kprev.shape, F32)
        vprev[...] = jnp.zeros(vprev.shape, F32)

    cos, sa, sb = cos_ref[...], sa_ref[...], sb_ref[...]
    q = _rope_flat(q_ref[...].astype(F32), cos, sa, sb) * (HEAD_DIM ** -0.5)
    k = _rope_flat(k_ref[...].astype(F32), cos, sa, sb)
    v = v_ref[...].astype(F32)
    kctx = jnp.concatenate([kprev[...], k], axis=0).astype(BF16)
    vctx = jnp.concatenate([vprev[...], v], axis=0).astype(BF16)
    rows = Q_PER_KV * w
    ti = lax.broadcasted_iota(jnp.int32, (rows, 2 * w), 0) % w
    ci = lax.broadcasted_iota(jnp.int32, (rows, 2 * w), 1)
    valid = (ci >= ti) & (ci <= ti + w) & ((ci >= w) | (j > 0))
    ones = jnp.ones((2 * w, LANE), BF16)
    qb = q.astype(BF16)
    outs = []
    for kv in range(KV_HEADS):
        kj = kctx[:, kv * HEAD_DIM:(kv + 1) * HEAD_DIM]
        vj = vctx[:, kv * HEAD_DIM:(kv + 1) * HEAD_DIM]
        heads = range(kv * Q_PER_KV, (kv + 1) * Q_PER_KV)
        qs = jnp.concatenate([qb[:, h * HEAD_DIM:(h + 1) * HEAD_DIM] for h in heads], axis=0)
        sink = jnp.concatenate([jnp.broadcast_to(sink_ref[0:1, h:h + 1], (w, 1)) for h in heads], axis=0)
        s = lax.dot_general(qs, kj, (((1,), (1,)), ((), ())), preferred_element_type=F32)
        s = jnp.where(valid, s, -jnp.inf)
        m = jnp.maximum(jnp.max(s, axis=-1, keepdims=True), sink)
        p = jnp.exp(s - m).astype(BF16)
        den = jnp.dot(p, ones, preferred_element_type=F32)[:, 0:HEAD_DIM] + jnp.exp(sink - m)
        o = jnp.dot(p, vj, preferred_element_type=F32) / den
        outs.extend(o[i * w:(i + 1) * w] for i in range(Q_PER_KV))
    o = jnp.concatenate(outs, axis=1)
    y_ref[...] = jnp.dot(o.astype(BF16), wo_ref[...], preferred_element_type=F32).astype(y_ref.dtype)
    kprev[...] = k
    vprev[...] = v

    @pl.when(j == nb - 1)
    def _():
        kn_ref[...] = k
        vn_ref[...] = v


def _attn_prompt(proj, cos, sa, sb, sinks, attn_o):
    b, l, _ = proj.shape
    d = D_MODEL
    w = WINDOW
    kvw = KV_HEADS * HEAD_DIM
    const2 = lambda i, j: (0, 0)
    tmap = lambda i, j: (j, 0)
    return pl.pallas_call(
        _attn_prompt_body,
        grid=(b, l // w),
        in_specs=[pl.BlockSpec((None, w, d), lambda i, j: (i, j, P_Q // d)),
                  pl.BlockSpec((None, w, kvw), lambda i, j: (i, j, P_K // kvw)),
                  pl.BlockSpec((None, w, kvw), lambda i, j: (i, j, P_V // kvw)),
                  pl.BlockSpec((w, LANE), tmap), pl.BlockSpec((w, LANE), tmap), pl.BlockSpec((w, LANE), tmap),
                  pl.BlockSpec((1, LANE), const2),
                  pl.BlockSpec((d, d), const2)],
        out_specs=[pl.BlockSpec((None, w, d), lambda i, j: (i, j, 0)),
                   pl.BlockSpec((None, w, kvw), lambda i, j: (i, 0, 0)),
                   pl.BlockSpec((None, w, kvw), lambda i, j: (i, 0, 0))],
        out_shape=[_sds((b, l, d), BF16), _sds((b, w, kvw)), _sds((b, w, kvw))],
        scratch_shapes=[pltpu.VMEM((w, kvw), F32), pltpu.VMEM((w, kvw), F32)],
        compiler_params=_cparams(2),
        name="attn_prompt",
    )(proj, proj, proj, cos, sa, sb, sinks, attn_o)


SAMPLE_TB = 8


def _mamba_sample_body(x_ref, bc_ref, dt_ref, cst_ref, ssm_ref, cw_ref, cb_ref, dtb_ref, alog_ref, dskip_ref,
                       exp_ref, stack_ref, y_ref, cnew_ref, ssmo_ref):
    del stack_ref
    tb = SAMPLE_TB
    new = jnp.concatenate([x_ref[...], bc_ref[...]], axis=1)
    conv = (cb_ref[...] + cw_ref[0:1, :] * cst_ref[0] + cw_ref[1:2, :] * cst_ref[1] + cw_ref[2:3, :] * cst_ref[2]
            + cw_ref[3:4, :] * new)
    cnew_ref[0] = cst_ref[1]
    cnew_ref[1] = cst_ref[2]
    cnew_ref[2] = new
    xbc = _silu(conv)
    xs = xbc[:, 0:M_INNER]
    gn = M_GROUPS * M_STATE
    bm = xbc[:, M_INNER:M_INNER + gn]
    cm = xbc[:, M_INNER + gn:].astype(BF16)
    dt = _softplus(dt_ref[...] + dtb_ref[...])
    a = dt * (-jnp.exp(alog_ref[...]))
    expand = exp_ref[...]
    dtx = _dot3_right(dt, expand) * xs
    da_x = jnp.exp(_dot3_right(a, expand))
    nl = M_INNER // LANE
    slab = jnp.concatenate([dtx[:, j * LANE:(j + 1) * LANE] for j in range(nl)]
                           + [da_x[:, j * LANE:(j + 1) * LANE] for j in range(nl)], axis=0)
    tt = slab.T
    hpg = M_HEADS // M_GROUPS
    gw = hpg * M_HEADDIM
    rowid = lax.broadcasted_iota(jnp.int32, (tb, gw), 0)
    ys = []
    for g in range(M_GROUPS):
        yacc = jnp.zeros((tb, gw), F32)
        for t in range(tb):
            parts = []
            for hl in range(hpg):
                h = g * hpg + hl
                j, hh = divmod(h * M_HEADDIM, LANE)
                col = j * tb + t
                dcol = tt[hh:hh + M_HEADDIM, col:col + 1]
                acol = tt[hh:hh + M_HEADDIM, nl * tb + col:nl * tb + col + 1]
                hn = acol * ssm_ref[t, h] + dcol * bm[t:t + 1, g * M_STATE:(g + 1) * M_STATE]
                ssmo_ref[t, h] = hn
                parts.append(hn)
            hng = jnp.concatenate(parts, axis=0).astype(BF16)
            yg = lax.dot_general(cm[:, g * M_STATE:(g + 1) * M_STATE], hng, (((1,), (1,)), ((), ())),
                                 preferred_element_type=F32)
            yacc = jnp.where(rowid == t, yg, yacc)
        ys.append(yacc)
    y_ref[...] = jnp.concatenate(ys, axis=1) + xs * dskip_ref[...]


def _mamba_sample(proj, dtp, conv_t, ssm, conv_w, conv_b, dt_bias, a_log, d_skip, expand, ssm_stack, li):
    s = proj.shape[0]
    d = D_MODEL
    tb = SAMPLE_TB
    const = lambda i: (0, 0)
    return pl.pallas_call(
        _mamba_sample_body,
        grid=(s // tb,),
        in_specs=[pl.BlockSpec((tb, d), lambda i: (i, S_X // d)),
                  pl.BlockSpec((tb, d), lambda i: (i, S_BC // d)),
                  pl.BlockSpec((tb, LANE), lambda i: (i, 0)),
                  pl.BlockSpec((CONV_W - 1, tb, CONV_DIM), lambda i: (0, i, 0)),
                  pl.BlockSpec((tb, M_HEADS, M_HEADDIM, M_STATE), lambda i: (i, 0, 0, 0)),
                  pl.BlockSpec((CONV_W, CONV_DIM), const), pl.BlockSpec((1, CONV_DIM), const),
                  pl.BlockSpec((1, LANE), const), pl.BlockSpec((1, LANE), const), pl.BlockSpec((1, d), const),
                  pl.BlockSpec((LANE, d), const),
                  pl.BlockSpec(memory_space=pl.ANY)],
        out_specs=[pl.BlockSpec((tb, d), lambda i: (i, 0)),
                   pl.BlockSpec((CONV_W - 1, tb, CONV_DIM), lambda i: (0, i, 0)),
                   pl.BlockSpec((None, tb, M_HEADS, M_HEADDIM, M_STATE), lambda i: (li, i, 0, 0, 0))],
        out_shape=[_sds((s, d)), _sds((CONV_W - 1, s, CONV_DIM)), _sds(ssm_stack.shape)],
        input_output_aliases={11: 2},
        compiler_params=_cparams(1),
        name="mamba_sample",
    )(proj, proj, dtp, conv_t, ssm, conv_w, conv_b, dt_bias, a_log, d_skip, expand, ssm_stack)


def _gate_norm_proj_body(y_ref, z_ref, nw_ref, w_ref, o_ref):
    y = _group_norm_gate(y_ref[...], z_ref[...], nw_ref[...])
    o_ref[...] = jnp.dot(y.astype(BF16), w_ref[...], preferred_element_type=F32)


def _gate_norm_proj(y, proj, m_norm_w, m_proj):
    s, d = y.shape
    const = lambda i: (0, 0)
    return pl.pallas_call(
        _gate_norm_proj_body,
        grid=(1,),
        in_specs=[pl.BlockSpec((s, d), const), pl.BlockSpec((s, d), lambda i: (0, S_Z // d)),
                  pl.BlockSpec((1, d), const), pl.BlockSpec((d, d), const)],
        out_specs=pl.BlockSpec((s, d), const),
        out_shape=_sds((s, d)),
        compiler_params=_cparams(1),
        name="gate_norm_proj",
    )(y, proj, m_norm_w, m_proj)


def _s5_sample_body(u_ref, hre_ref, him_ref, wbr_ref, wbi_ref, wcr_ref, wci_ref, tab_ref, d_ref, y_ref, ore_ref,
                    oim_ref):
    u = u_ref[...]
    ub = u.astype(BF16)
    bre = jnp.dot(ub, wbr_ref[...], preferred_element_type=F32)
    bim = jnp.dot(ub, wbi_ref[...], preferred_element_type=F32)
    ar = tab_ref[0]
    ai = tab_ref[1]
    h0r = hre_ref[...]
    h0i = him_ref[...]
    hr = ar * h0r - ai * h0i + bre
    hi = ar * h0i + ai * h0r + bim
    ore_ref[...] = hr
    oim_ref[...] = hi
    y = (jnp.dot(hr.astype(BF16), wcr_ref[...], preferred_element_type=F32)
         + jnp.dot(hi.astype(BF16), wci_ref[...], preferred_element_type=F32) + d_ref[...] * u)
    y_ref[...] = _gelu_tanh(y)


def _s5_sample(proj, h_re, h_im, wbr, wbi, wcr, wci, tab, d_skip):
    s = proj.shape[0]
    nb = S5_GROUPS // S5_LB
    ub = S_U // LANE
    wmap = lambda g: (g, 0, 0)
    hspec = pl.BlockSpec((s, S5_LBN), lambda g: (0, g))
    return pl.pallas_call(
        _s5_sample_body,
        grid=(nb,),
        in_specs=[pl.BlockSpec((s, LANE), lambda g: (0, ub + g)), hspec, hspec,
                  pl.BlockSpec((None, LANE, S5_LBN), wmap), pl.BlockSpec((None, LANE, S5_LBN), wmap),
                  pl.BlockSpec((None, S5_LBN, LANE), wmap), pl.BlockSpec((None, S5_LBN, LANE), wmap),
                  pl.BlockSpec((None, 2, 1, S5_LBN), lambda g: (g, 0, 0, 0)),
                  pl.BlockSpec((1, LANE), lambda g: (0, g))],
        out_specs=[pl.BlockSpec((None, s, LANE), wmap), hspec, hspec],
        out_shape=[_sds((nb, s, LANE)), _sds(h_re.shape), _sds(h_im.shape)],
        compiler_params=_cparams(1),
        name="s5_sample",
    )(proj, h_re, h_im, wbr, wbi, wcr, wci, tab, d_skip)


def _rope_sample_body(q_ref, k_ref, cos_ref, sa_ref, sb_ref, qo_ref, ko_ref):
    cos, sa, sb = cos_ref[...], sa_ref[...], sb_ref[...]
    qo_ref[...] = _rope_flat(q_ref[...], cos, sa, sb) * (HEAD_DIM ** -0.5)
    ko_ref[...] = _rope_flat(k_ref[...], cos, sa, sb)


def _rope_sample(proj, cos, sa, sb):
    s = proj.shape[0]
    kvw = KV_HEADS * HEAD_DIM
    const = lambda i: (0, 0)
    return pl.pallas_call(
        _rope_sample_body,
        grid=(1,),
        in_specs=[pl.BlockSpec((s, QX), lambda i: (0, S_QX // QX)), pl.BlockSpec((s, kvw), lambda i: (0, S_K // kvw)),
                  pl.BlockSpec((1, LANE), const), pl.BlockSpec((1, LANE), const), pl.BlockSpec((1, LANE), const)],
        out_specs=[pl.BlockSpec((s, QX), const), pl.BlockSpec((s, kvw), const)],
        out_shape=[_sds((s, QX)), _sds((s, kvw))],
        compiler_params=_cparams(1),
        name="rope_sample",
    )(proj, proj, cos, sa, sb)


def _attn_sample_body(q_ref, kc_ref, vc_ref, kn_ref, vn_ref, sink_ref, kstack_ref, vstack_ref, o_ref, ko_ref, vo_ref):
    del kstack_ref, vstack_ref
    q = q_ref[...]
    kc = kc_ref[...]
    vc = vc_ref[...]
    kn = kn_ref[...]
    vn = vn_ref[...]
    w = kc.shape[1]
    s_c = lax.dot_general(q.astype(BF16), kc.astype(BF16), (((2,), (2,)), ((0,), (0,))),
                          preferred_element_type=F32)
    s_n = jnp.sum(q * kn, axis=-1, keepdims=True)
    sink = sink_ref[...][None]
    m = jnp.maximum(jnp.maximum(jnp.max(s_c, axis=-1, keepdims=True), s_n), sink)
    p_c = jnp.exp(s_c - m)
    p_n = jnp.exp(s_n - m)
    den = jnp.sum(p_c, axis=-1, keepdims=True) + p_n + jnp.exp(sink - m)
    o = lax.dot_general((p_c / den).astype(BF16), vc.astype(BF16), (((2,), (1,)), ((0,), (0,))),
                        preferred_element_type=F32)
    o_ref[...] = o + (p_n / den) * vn
    ko_ref[:, 0:w - 1, :] = kc[:, 1:w, :]
    ko_ref[:, w - 1:w, :] = kn
    vo_ref[:, 0:w - 1, :] = vc[:, 1:w, :]
    vo_ref[:, w - 1:w, :] = vn


def _attn_sample(qx, kc, vc, kn, proj3, sinks, k_stack, v_stack, li):
    s, w, kvw = kc.shape
    tb = SAMPLE_TB
    row3 = lambda i: (i, 0, 0)
    slab = pl.BlockSpec((None, tb, w, kvw), lambda i: (li, i, 0, 0))
    return pl.pallas_call(
        _attn_sample_body,
        grid=(s // tb,),
        in_specs=[pl.BlockSpec((tb, A_HEADS, kvw), row3), pl.BlockSpec((tb, w, kvw), row3),
                  pl.BlockSpec((tb, w, kvw), row3), pl.BlockSpec((tb, 1, kvw), row3),
                  pl.BlockSpec((tb, 1, kvw), lambda i: (i, 0, S_V // kvw)),
                  pl.BlockSpec((A_HEADS, 1), lambda i: (0, 0)),
                  pl.BlockSpec(memory_space=pl.ANY), pl.BlockSpec(memory_space=pl.ANY)],
        out_specs=[pl.BlockSpec((tb, A_HEADS, kvw), row3), slab, slab],
        out_shape=[_sds((s, A_HEADS, kvw)), _sds(k_stack.shape), _sds(v_stack.shape)],
        input_output_aliases={6: 1, 7: 2},
        compiler_params=_cparams(1),
        name="attn_sample",
    )(qx, kc, vc, kn, proj3, sinks, k_stack, v_stack)


def _dense_body(x_ref, w_ref, o_ref):
    o_ref[...] = jnp.dot(x_ref[...].astype(BF16), w_ref[...], preferred_element_type=F32)


def _dense(x, w, tn):
    m, k = x.shape
    n = w.shape[1]
    return pl.pallas_call(
        _dense_body,
        grid=(n // tn,),
        in_specs=[pl.BlockSpec((m, k), lambda j: (0, 0)), pl.BlockSpec((k, tn), lambda j: (0, j))],
        out_specs=pl.BlockSpec((m, tn), lambda j: (0, j)),
        out_shape=_sds((m, n)),
        compiler_params=_cparams(1),
        name="dense",
    )(x, w)


def _pad_lanes(v):
    return jnp.zeros((1, LANE), F32).at[0, :v.shape[0]].set(v.astype(F32))


def _layer_weights(l, w_in, attn_o):
    w = w_in[l]
    seg = lambda a, b: w[:, a:b]
    z, xbc, dt, u = seg(OFF_Z, OFF_XBC), seg(OFF_XBC, OFF_DT), seg(OFF_DT, OFF_U), seg(OFF_U, OFF_Q)
    q, k, v, g = seg(OFF_Q, OFF_K), seg(OFF_K, OFF_V), seg(OFF_V, OFF_G), seg(OFF_G, IN_COLS)
    w_prompt = jnp.concatenate([z, xbc, u, q, g, k, v], axis=1).astype(BF16)
    sel = (jnp.arange(A_HEADS)[:, None] // Q_PER_KV == jnp.arange(KV_HEADS)[None, :]).astype(F32)
    qx = (q.reshape(D_MODEL, A_HEADS, 1, HEAD_DIM) * sel[None, :, :, None]).reshape(D_MODEL, QX)
    w_sample = jnp.concatenate([z, xbc, u, qx, g, k, v], axis=1).astype(BF16)
    w_dt = jnp.zeros((D_MODEL, LANE), F32).at[:, :M_HEADS].set(dt).astype(BF16)
    ao = attn_o[l].reshape(A_HEADS, 1, HEAD_DIM, D_MODEL) * sel[:, :, None, None]
    return w_prompt, w_sample, w_dt, ao.reshape(QX, D_MODEL).astype(BF16)


def kernel(x_prompt, x_sample, state_ssm, state_conv, state_s5_re, state_s5_im, cache_k, cache_v, norm1_w, w_in,
           conv_w, conv_b, dt_bias, a_log, m_d, m_norm_w, m_proj, s5_lam_re, s5_lam_im, s5_log_step, s5_b_re,
           s5_b_im, s5_c_re, s5_c_im, s5_d, s5_glu_w, attn_sinks, attn_o, w_out, norm2_w, mlp_up, mlp_down,
           final_norm_w):
    b, l, d = x_prompt.shape
    s = x_sample.shape[0]
    kvw = KV_HEADS * HEAD_DIM
    nb = S5_GROUPS // S5_LB
    xp = x_prompt.reshape(b * l, d)
    xs = x_sample.reshape(s, d)
    cos_p, sa_p, sb_p = _rope_tables(jnp.arange(l, dtype=jnp.int32))
    cos_s, sa_s, sb_s = _rope_tables(jnp.full((1,), PAST_LEN, jnp.int32))
    expand = (jnp.arange(LANE)[:, None] == jnp.arange(M_INNER)[None, :] // M_HEADDIM).astype(BF16)
    fnw = final_norm_w[None]
    ssm_s = jnp.zeros((DEPTH,) + state_ssm.shape[1:], F32)
    k_s = jnp.zeros((DEPTH, s, WINDOW, kvw), F32)
    v_s = jnp.zeros((DEPTH, s, WINDOW, kvw), F32)
    outs = [[] for _ in range(9)]
    for li in range(DEPTH):
        final = li == DEPTH - 1
        w_prompt, w_sample, w_dt, ao_x = _layer_weights(li, w_in, attn_o)
        n1 = norm1_w[li][None]
        cw, cb = conv_w[li], conv_b[li][None]
        dtb, alog = _pad_lanes(dt_bias[li]), _pad_lanes(a_log[li])
        dsk = jnp.repeat(m_d[li], M_HEADDIM)[None]
        mnw = m_norm_w[li][None]
        mpj = m_proj[li].astype(BF16)
        s5_chunked, s5_step = _s5_operators(s5_lam_re[li], s5_lam_im[li], s5_log_step[li], s5_b_re[li], s5_b_im[li],
                                            s5_c_re[li], s5_c_im[li])
        s5d = s5_d[li][None]
        sinks = attn_sinks[li]
        ao = attn_o[li].astype(BF16)
        mlp_w = (s5_glu_w[li].astype(BF16), w_out[li].astype(BF16), norm2_w[li][None], mlp_up[li].astype(BF16),
                 mlp_down[li].astype(BF16), fnw)

        proj, dtp = _norm_proj(xp, n1, w_prompt, w_dt, 2048, 512, BF16)
        proj3 = proj.reshape(b, l, P_COLS)
        ym, conv_p, ssm_p = _mamba_prompt(proj3, dtp.reshape(b, l, LANE), cw, cb, dtb, alog, dsk, mnw, expand, mpj)
        ys, h_p = _s5_prompt(proj3, *s5_chunked, s5d, 2048)
        hre_p, him_p = h_p[:, :, 0, :S5_LBN], h_p[:, :, 0, S5_LBN:]
        ya, k_p, v_p = _attn_prompt(proj3, cos_p, sa_p, sb_p, _pad_lanes(sinks), ao)
        xp = _merge_mlp(xp, proj, P_G, ym.reshape(b * l, d), ys.reshape(nb, b * l, LANE), ya.reshape(b * l, d),
                        *mlp_w, 512, final)

        sproj, sdt = _norm_proj(xs, n1, w_sample, w_dt, s, 512, F32)
        y_pre, conv_t, ssm_s = _mamba_sample(sproj, sdt, state_conv[li].transpose(1, 0, 2), state_ssm[li], cw, cb,
                                             dtb, alog, dsk, expand, ssm_s, li)
        sym = _gate_norm_proj(y_pre, sproj, mnw, mpj)
        sys_, hre_s, him_s = _s5_sample(sproj, state_s5_re[li].reshape(s, S5_GROUPS * S5_STATE),
                                        state_s5_im[li].reshape(s, S5_GROUPS * S5_STATE), *s5_step, s5d)
        qx_rot, k_rot = _rope_sample(sproj, cos_s, sa_s, sb_s)
        o, k_s, v_s = _attn_sample(qx_rot.reshape(s, A_HEADS, kvw), cache_k[li].reshape(s, WINDOW, kvw),
                                   cache_v[li].reshape(s, WINDOW, kvw), k_rot.reshape(s, 1, kvw),
                                   sproj.reshape(s, 1, S_COLS), sinks[:, None], k_s, v_s, li)
        sya = _dense(o.reshape(s, QX), ao_x, 512)
        xs = _merge_mlp(xs, sproj, S_G, sym, sys_, sya, *mlp_w, s, final)

        for i, val in enumerate((
                ssm_p, conv_p, conv_t.transpose(1, 0, 2),
                hre_p.reshape(b, S5_GROUPS, S5_STATE), hre_s.reshape(s, S5_GROUPS, S5_STATE),
                him_p.reshape(b, S5_GROUPS, S5_STATE), him_s.reshape(s, S5_GROUPS, S5_STATE),
                k_p.reshape(b, WINDOW, KV_HEADS, HEAD_DIM), v_p.reshape(b, WINDOW, KV_HEADS, HEAD_DIM))):
            outs[i].append(val)
    ssm_p, conv_p, conv_s, hre_p, hre_s, him_p, him_s, k_p, v_p = (jnp.stack(o) for o in outs)
    cache_shape = (DEPTH, s, WINDOW, KV_HEADS, HEAD_DIM)
    return (xp.reshape(b, l, d), xs.reshape(s, 1, d), ssm_p, ssm_s, conv_p, conv_s, hre_p, hre_s, him_p, him_s,
            k_p, k_s.reshape(cache_shape), v_p, v_s.reshape(cache_shape))
```

```python
import functools
import math

import jax
import jax.numpy as jnp
from jax import lax
from jax.experimental import pallas as pl
from jax.experimental.pallas import tpu as pltpu

F32 = jnp.float32
BF16 = jnp.bfloat16

D_MODEL = 1024
DEPTH = 4
PAST_LEN = 8192
M_HEADDIM = 64
M_HEADS = 16
M_INNER = 1024
M_GROUPS = 4
M_STATE = 128
CONV_W = 4
CONV_DIM = 2048
CHUNK = 128
S5_GROUPS = 64
S5_GSIZE = 16
S5_STATE = 64
HEAD_DIM = 64
A_HEADS = 16
KV_HEADS = 4
Q_PER_KV = 4
ROT_DIM = 16
ROPE_THETA = 500000.0
WINDOW = 128
D_FF = 4096
EPS = 1e-6

OFF_Z, OFF_XBC, OFF_DT, OFF_U, OFF_Q, OFF_K, OFF_V, OFF_G, IN_COLS = 0, 1024, 3072, 3088, 4112, 5136, 5392, 5648, 8720

LANE = 128
SUBLANE = 8
S5_LB = 8
S5_LBN = S5_LB * S5_STATE
VMEM_LIMIT = 56 * 1024 * 1024

P_Z, P_X, P_BC, P_U, P_Q, P_G, P_K, P_V, P_COLS = 0, 1024, 2048, 3072, 4096, 5120, 8192, 8448, 8704
S_Z, S_X, S_BC, S_U, S_QX, S_G, S_K, S_V, S_COLS = 0, 1024, 2048, 3072, 4096, 8192, 11264, 11520, 11776
QX = A_HEADS * KV_HEADS * HEAD_DIM


def _sds(shape, dtype=F32):
    return jax.ShapeDtypeStruct(shape, dtype)


def _cparams(n_axes):
    return pltpu.CompilerParams(dimension_semantics=("arbitrary",) * n_axes, vmem_limit_bytes=VMEM_LIMIT)


def _sigmoid(x):
    return 1.0 / (1.0 + jnp.exp(-x))


def _silu(x):
    return x * _sigmoid(x)


def _softplus(x):
    return jnp.maximum(x, 0.0) + jnp.log(1.0 + jnp.exp(-jnp.abs(x)))


def _gelu_tanh(x):
    return 0.5 * x * (1.0 + jnp.tanh(math.sqrt(2.0 / math.pi) * (x + 0.044715 * (x * x * x))))


def _bdot(a, b):
    return jnp.dot(a.astype(BF16), b.astype(BF16), preferred_element_type=F32)


def _split3(a):
    hi = a.astype(BF16)
    r = a - hi.astype(F32)
    mid = r.astype(BF16)
    lo = (r - mid.astype(F32)).astype(BF16)
    return hi, mid, lo


def _dot3_right(a, sel):
    hi, mid, lo = _split3(a)
    return (jnp.dot(hi, sel, preferred_element_type=F32) + jnp.dot(mid, sel, preferred_element_type=F32)
            + jnp.dot(lo, sel, preferred_element_type=F32))


def _dot3_left(sel, a):
    hi, mid, lo = _split3(a)
    return (jnp.dot(sel, hi, preferred_element_type=F32) + jnp.dot(sel, mid, preferred_element_type=F32)
            + jnp.dot(sel, lo, preferred_element_type=F32))


def _rms(x, w):
    return x * lax.rsqrt(jnp.mean(x * x, axis=-1, keepdims=True) + EPS) * w


def _norm_proj_body(x_ref, nw_ref, w_ref, wdt_ref, o_ref, odt_ref, h_scr):
    @pl.when(pl.program_id(1) == 0)
    def _():
        hb = _rms(x_ref[...], nw_ref[...]).astype(BF16)
        h_scr[...] = hb
        odt_ref[...] = jnp.dot(hb, wdt_ref[...], preferred_element_type=F32)

    o_ref[...] = jnp.dot(h_scr[...], w_ref[...], preferred_element_type=F32).astype(o_ref.dtype)


def _norm_proj(x, nw, w, wdt, tm, tn, out_dtype):
    m, d = x.shape
    n = w.shape[1]
    return pl.pallas_call(
        _norm_proj_body,
        grid=(m // tm, n // tn),
        in_specs=[pl.BlockSpec((tm, d), lambda i, j: (i, 0)),
                  pl.BlockSpec((1, d), lambda i, j: (0, 0)),
                  pl.BlockSpec((d, tn), lambda i, j: (0, j)),
                  pl.BlockSpec((d, LANE), lambda i, j: (0, 0))],
        out_specs=[pl.BlockSpec((tm, tn), lambda i, j: (i, j)),
                   pl.BlockSpec((tm, LANE), lambda i, j: (i, 0))],
        out_shape=[_sds((m, n), out_dtype), _sds((m, LANE))],
        scratch_shapes=[pltpu.VMEM((tm, d), BF16)],
        compiler_params=_cparams(2),
        name="norm_proj",
    )(x, nw, w, wdt)


def _merge_mlp_body(final, x_ref, g0_ref, g1_ref, g2_ref, ym_ref, ys_ref, ya_ref, glu_ref, wo_ref, n2_ref, up_ref,
                    dn_ref, fn_ref, o_ref):
    ys = jnp.concatenate([ys_ref[g] for g in range(ys_ref.shape[0])], axis=1)
    glu = jnp.dot(ys.astype(BF16), glu_ref[...], preferred_element_type=F32)
    y_s = glu[:, :D_MODEL] * _sigmoid(glu[:, D_MODEL:])
    f32 = lambda r: r[...].astype(F32)
    merged = _sigmoid(f32(g0_ref)) * f32(ym_ref) + _sigmoid(f32(g1_ref)) * y_s + _sigmoid(f32(g2_ref)) * f32(ya_ref)
    x1 = x_ref[...] + jnp.dot(merged.astype(BF16), wo_ref[...], preferred_element_type=F32)
    h2 = _rms(x1, n2_ref[...]).astype(BF16)
    acc = x1
    fc = 1024
    for c in range(D_FF // fc):
        a = jnp.dot(h2, up_ref[:, c * fc:(c + 1) * fc], preferred_element_type=F32)
        a = jnp.square(jnp.maximum(a, 0.0))
        acc = acc + jnp.dot(a.astype(BF16), dn_ref[c * fc:(c + 1) * fc, :], preferred_element_type=F32)
    o_ref[...] = _rms(acc, fn_ref[...]) if final else acc


def _merge_mlp(x, proj, g_col, ym, ys, ya, glu_w, w_out, n2, up, dn, fnw, tm, final):
    m, d = x.shape
    gb = g_col // d
    nlb = d // LANE
    row = lambda i: (i, 0)
    const = lambda i: (0, 0)
    wspec = lambda shape: pl.BlockSpec(shape, const, pipeline_mode=pl.Buffered(1))
    return pl.pallas_call(
        functools.partial(_merge_mlp_body, final),
        grid=(m // tm,),
        in_specs=[pl.BlockSpec((tm, d), row),
                  pl.BlockSpec((tm, d), lambda i: (i, gb)),
                  pl.BlockSpec((tm, d), lambda i: (i, gb + 1)),
                  pl.BlockSpec((tm, d), lambda i: (i, gb + 2)),
                  pl.BlockSpec((tm, d), row), pl.BlockSpec((nlb, tm, LANE), lambda i: (0, i, 0)),
                  pl.BlockSpec((tm, d), row),
                  wspec((d, 2 * d)), wspec((d, d)), wspec((1, d)), wspec((d, D_FF)), wspec((D_FF, d)),
                  wspec((1, d))],
        out_specs=pl.BlockSpec((tm, d), row),
        out_shape=_sds((m, d)),
        compiler_params=_cparams(1),
        name="merge_mlp",
    )(x, proj, proj, proj, ym, ys, ya, glu_w, w_out, n2, up, dn, fnw)


def _group_norm_gate(y, z, nw):
    y = y * _silu(z)
    gw = M_INNER // M_GROUPS
    parts = []
    for g in range(M_GROUPS):
        yg = y[:, g * gw:(g + 1) * gw]
        parts.append(yg * lax.rsqrt(jnp.mean(yg * yg, axis=-1, keepdims=True) + EPS))
    return jnp.concatenate(parts, axis=1) * nw


def _mamba_prompt_body(z_ref, x_ref, bc_ref, dt_ref, cw_ref, cb_ref, dtb_ref, alog_ref, dskip_ref, nw_ref,
                       exp_ref, mproj_ref, y_ref, conv_ref, ssm_ref, xprev, ht):
    c = pl.program_id(1)
    nc = pl.num_programs(1)
    q = CHUNK
    tail = xprev.shape[0]

    @pl.when(c == 0)
    def _():
        xprev[...] = jnp.zeros(xprev.shape, xprev.dtype)
        ht[...] = jnp.zeros(ht.shape, F32)

    xcur = jnp.concatenate([x_ref[...], bc_ref[...]], axis=1)
    xext = jnp.concatenate([xprev[...], xcur], axis=0)
    ri = lax.broadcasted_iota(jnp.int32, (q, tail + q), 0)
    ci = lax.broadcasted_iota(jnp.int32, (q, tail + q), 1)
    conv = cb_ref[...] + cw_ref[CONV_W - 1:CONV_W, :] * xcur.astype(F32)
    for k in range(CONV_W - 1):
        shift = (ci == ri + (tail - (CONV_W - 1) + k)).astype(BF16)
        conv = conv + cw_ref[k:k + 1, :] * jnp.dot(shift, xext, preferred_element_type=F32)
    last_rows = xcur[q - tail:q, :]
    xprev[...] = last_rows

    @pl.when(c == nc - 1)
    def _():
        conv_ref[...] = last_rows.astype(F32)[tail - (CONV_W - 1):tail, :]

    xbc = _silu(conv)
    xs = xbc[:, 0:M_INNER]
    gn = M_GROUPS * M_STATE
    bm = xbc[:, M_INNER:M_INNER + gn]
    cm = xbc[:, M_INNER + gn:]

    dt = _softplus(dt_ref[...] + dtb_ref[...])
    a = dt * (-jnp.exp(alog_ref[...]))
    ri = lax.broadcasted_iota(jnp.int32, (q, q), 0)
    ci = lax.broadcasted_iota(jnp.int32, (q, q), 1)
    causal = ci <= ri
    tri = causal.astype(BF16)
    acum = _dot3_left(tri, a)
    expand = exp_ref[...]
    acum_x = _dot3_right(acum, expand)
    dt_x = _dot3_right(dt, expand)
    acum_last = acum_x[q - 1:q, :]
    xdt = xs * dt_x
    xw = xs * (jnp.exp(acum_last - acum_x) * dt_x)
    e_acum = jnp.exp(acum_x)
    chunk_decay = jnp.exp(acum_last)
    acum_t = acum.T

    hpg = M_HEADS // M_GROUPS
    gw = hpg * M_HEADDIM
    y_parts = []
    for g in range(M_GROUPS):
        cg = cm[:, g * M_STATE:(g + 1) * M_STATE].astype(BF16)
        bg = bm[:, g * M_STATE:(g + 1) * M_STATE]
        bgb = bg.astype(BF16)
        cb = lax.dot_general(cg, bgb, (((1,), (1,)), ((), ())), preferred_element_type=F32)
        h_prev = ht[g]
        y_off = jnp.dot(cg, h_prev.astype(BF16), preferred_element_type=F32) * e_acum[:, g * gw:(g + 1) * gw]
        yd = []
        for hl in range(hpg):
            h = g * hpg + hl
            seg = acum[:, h:h + 1] - acum_t[h:h + 1, :]
            lmat = jnp.exp(jnp.where(causal, seg, -jnp.inf))
            mh = (cb * lmat).astype(BF16)
            yd.append(jnp.dot(mh, xdt[:, h * M_HEADDIM:(h + 1) * M_HEADDIM].astype(BF16),
                              preferred_element_type=F32))
        y_parts.append(jnp.concatenate(yd, axis=1) + y_off)
        s_new = jnp.dot(bg.T.astype(BF16), xw[:, g * gw:(g + 1) * gw].astype(BF16), preferred_element_type=F32)
        ht[g] = h_prev * chunk_decay[:, g * gw:(g + 1) * gw] + s_new
    y = jnp.concatenate(y_parts, axis=1) + xs * dskip_ref[...]
    y = _group_norm_gate(y, z_ref[...].astype(F32), nw_ref[...])
    y_ref[...] = jnp.dot(y.astype(BF16), mproj_ref[...], preferred_element_type=F32).astype(y_ref.dtype)

    @pl.when(c == nc - 1)
    def _():
        for g in range(M_GROUPS):
            ssm_ref[g * hpg:(g + 1) * hpg] = ht[g].T.reshape(hpg, M_HEADDIM, M_STATE)


def _mamba_prompt(proj, dtp, conv_w, conv_b, dt_bias, a_log, d_skip, m_norm_w, expand, m_proj):
    b, l, _ = proj.shape
    d = D_MODEL
    nc = l // CHUNK
    const2 = lambda i, j: (0, 0)
    return pl.pallas_call(
        _mamba_prompt_body,
        grid=(b, nc),
        in_specs=[pl.BlockSpec((None, CHUNK, d), lambda i, j: (i, j, P_Z // d)),
                  pl.BlockSpec((None, CHUNK, d), lambda i, j: (i, j, P_X // d)),
                  pl.BlockSpec((None, CHUNK, d), lambda i, j: (i, j, P_BC // d)),
                  pl.BlockSpec((None, CHUNK, LANE), lambda i, j: (i, j, 0)),
                  pl.BlockSpec((CONV_W, CONV_DIM), const2),
                  pl.BlockSpec((1, CONV_DIM), const2),
                  pl.BlockSpec((1, LANE), const2),
                  pl.BlockSpec((1, LANE), const2),
                  pl.BlockSpec((1, d), const2),
                  pl.BlockSpec((1, d), const2),
                  pl.BlockSpec((LANE, d), const2),
                  pl.BlockSpec((d, d), const2)],
        out_specs=[pl.BlockSpec((None, CHUNK, d), lambda i, j: (i, j, 0)),
                   pl.BlockSpec((None, CONV_W - 1, CONV_DIM), lambda i, j: (i, 0, 0)),
                   pl.BlockSpec((None, M_HEADS, M_HEADDIM, M_STATE), lambda i, j: (i, 0, 0, 0))],
        out_shape=[_sds((b, l, d), BF16), _sds((b, CONV_W - 1, CONV_DIM)), _sds((b, M_HEADS, M_HEADDIM, M_STATE))],
        scratch_shapes=[pltpu.VMEM((2 * SUBLANE, CONV_DIM), BF16),
                        pltpu.VMEM((M_GROUPS, M_STATE, (M_HEADS // M_GROUPS) * M_HEADDIM), F32)],
        compiler_params=_cparams(2),
        name="mamba_prompt",
    )(proj, proj, proj, dtp, conv_w, conv_b, dt_bias, a_log, d_skip, m_norm_w, expand, m_proj)


S5_Q = 8


def _cmul(a, b):
    return a[0] * b[0] - a[1] * b[1], a[0] * b[1] + a[1] * b[0]


def _s5_disc_body(lr_ref, li_ref, ls_ref, bre_ref, bim_ref, cre_ref, cim_ref, p1re_ref, p1im_ref, pqre_ref, pqim_ref,
                  bare_ref, baim_ref, care_ref, caim_ref, ckre_ref, ckim_ref):
    ng = lr_ref.shape[0]
    gs = S5_GSIZE
    lr = lr_ref[...]
    li = li_ref[...]
    step = jnp.exp(ls_ref[...])
    mag = jnp.exp(lr * step)
    ab = (mag * jnp.cos(li * step), mag * jnp.sin(li * step))
    den = lr * lr + li * li
    nr = ab[0] - 1.0
    ni = ab[1]
    f = ((nr * lr + ni * li) / den, (ni * lr - nr * li) / den)
    p1re_ref[...] = ab[0]
    p1im_ref[...] = ab[1]
    pows = [(jnp.ones_like(lr), jnp.zeros_like(lr))]
    for _ in range(S5_Q):
        pows.append(_cmul(pows[-1], ab))
    r = pows[S5_Q]
    for k in range(SUBLANE):
        pqre_ref[k] = r[0]
        pqim_ref[k] = r[1]
        r = _cmul(r, pows[S5_Q])
    for i in range(S5_GSIZE):
        bb = _cmul(f, (bre_ref[i], bim_ref[i]))
        chan = pl.ds(i, ng, stride=gs)
        for s in range(S5_Q):
            v = _cmul(pows[S5_Q - 1 - s], bb)
            bare_ref[s, chan, :] = v[0]
            baim_ref[s, chan, :] = v[1]
        c = (cre_ref[i], cim_ref[i])
        for k in range(S5_Q + 1):
            v = _cmul(c, pows[k])
            care_ref[k, chan, :] = v[0]
            caim_ref[k, chan, :] = v[1]
            if k < S5_Q:
                lagged = pl.ds(k * gs + i, ng, stride=S5_Q * gs)
                ckre_ref[lagged, :] = v[0]
                ckim_ref[lagged, :] = v[1]


def _s5_disc(lam_re, lam_im, log_step, b_re_t, b_im_t, c_re_t, c_im_t):
    g, n = lam_re.shape
    gn = _sds((g, n))
    pq = _sds((SUBLANE, g, n))
    ba = _sds((S5_Q, g * S5_GSIZE, n))
    ca = _sds((S5_Q + 1, g * S5_GSIZE, n))
    ck = _sds((g * S5_Q * S5_GSIZE, n))
    return pl.pallas_call(
        _s5_disc_body,
        out_shape=[gn, gn, pq, pq, ba, ba, ca, ca, ck, ck],
        name="s5_disc",
    )(lam_re, lam_im, log_step, b_re_t, b_im_t, c_re_t, c_im_t)


def _hi_lo(a):
    hi = a.astype(BF16)
    return hi, (a - hi.astype(F32)).astype(BF16)


def _s5_kmat_body(bbr_ref, bbi_ref, ckr_ref, cki_ref, k_ref):
    dn = (((2,), (2,)), ((0,), (0,)))

    def mm(a, b):
        ah, al = _hi_lo(a)
        bh, bl = _hi_lo(b)
        d = lambda x, y: lax.dot_general(x, y, dn, preferred_element_type=F32)
        return d(ah, bh) + d(ah, bl) + d(al, bh)

    k_ref[...] = mm(bbr_ref[...], ckr_ref[...]) - mm(bbi_ref[...], cki_ref[...])


def _s5_kmat(bb_re, bb_im, ck_re, ck_im):
    g, m, _ = ck_re.shape
    return pl.pallas_call(
        _s5_kmat_body,
        out_shape=_sds((g, S5_GSIZE, m)),
        name="s5_kmat",
    )(bb_re, bb_im, ck_re, ck_im)


def _block_diag_lanes(a, width):
    rows, total = LANE, S5_LB * width
    rep = (lax.broadcasted_iota(jnp.int32, (width, total), 1) % width
           == lax.broadcasted_iota(jnp.int32, (width, total), 0)).astype(BF16)
    tiled = jnp.dot(a.astype(BF16), rep, preferred_element_type=F32)
    same_group = (lax.broadcasted_iota(jnp.int32, (rows, total), 0) // S5_GSIZE
                  == lax.broadcasted_iota(jnp.int32, (rows, total), 1) // width)
    return jnp.where(same_group, tiled, 0.0).astype(BF16)


def _s5_build_body(kt_ref, bar_ref, bai_ref, car_ref, cai_ref, tin_ref, wst_ref, wot_ref, c0t_ref):
    q = S5_Q
    gs = S5_GSIZE
    kt = kt_ref[...]
    lag_blocks = [_block_diag_lanes(kt[:, k * gs:(k + 1) * gs], gs) for k in range(q)]
    zero = jnp.zeros((LANE, LANE), BF16)
    for s in range(q):
        rs = slice(s * LANE, (s + 1) * LANE)
        for t in range(q):
            tin_ref[rs, t * LANE:(t + 1) * LANE] = lag_blocks[t - s] if t >= s else zero
        wst_ref[rs, 0:S5_LBN] = _block_diag_lanes(bar_ref[s], S5_STATE)
        wst_ref[rs, S5_LBN:] = _block_diag_lanes(bai_ref[s], S5_STATE)
        wot_ref[rs, 0:S5_LBN] = _block_diag_lanes(car_ref[s + 1], S5_STATE)
        wot_ref[rs, S5_LBN:] = _block_diag_lanes(-cai_ref[s + 1], S5_STATE)
    c0t_ref[:, 0:S5_LBN] = _block_diag_lanes(car_ref[0], S5_STATE)
    c0t_ref[:, S5_LBN:] = _block_diag_lanes(-cai_ref[0], S5_STATE)


def _s5_build(kt, ba_re, ba_im, ca_re, ca_im):
    nb, q = S5_GROUPS // S5_LB, S5_Q
    blk = lambda n: pl.BlockSpec((n, None, LANE, S5_STATE), lambda g: (0, g, 0, 0))
    big = pl.BlockSpec((None, q * LANE, 2 * S5_LBN), lambda g: (g, 0, 0))
    return pl.pallas_call(
        _s5_build_body,
        grid=(nb,),
        in_specs=[pl.BlockSpec((None, LANE, q * S5_GSIZE), lambda g: (g, 0, 0)), blk(q), blk(q), blk(q + 1), blk(q + 1)],
        out_specs=[pl.BlockSpec((None, q * LANE, q * LANE), lambda g: (g, 0, 0)), big, big,
                   pl.BlockSpec((None, LANE, 2 * S5_LBN), lambda g: (g, 0, 0))],
        out_shape=[_sds((nb, q * LANE, q * LANE), BF16), _sds((nb, q * LANE, 2 * S5_LBN), BF16),
                   _sds((nb, q * LANE, 2 * S5_LBN), BF16), _sds((nb, LANE, 2 * S5_LBN), BF16)],
        compiler_params=_cparams(1),
        name="s5_build",
    )(kt, ba_re, ba_im, ca_re, ca_im)


def _s5_operators(lam_re, lam_im, log_step, b_re, b_im, c_re, c_im):
    nb, q = S5_GROUPS // S5_LB, S5_Q
    g, gs, n = S5_GROUPS, S5_GSIZE, S5_STATE
    p1_re, p1_im, pq_re, pq_im, ba_re, ba_im, ca_re, ca_im, ck_re, ck_im = _s5_disc(
        lam_re, lam_im, log_step[:, None], b_re.transpose(2, 0, 1), b_im.transpose(2, 0, 1),
        c_re.transpose(1, 0, 2), c_im.transpose(1, 0, 2))
    kt = _s5_kmat(ba_re[q - 1].reshape(g, gs, n), ba_im[q - 1].reshape(g, gs, n),
                  ck_re.reshape(g, q * gs, n), ck_im.reshape(g, q * gs, n))
    by_block = lambda a: a.reshape(a.shape[0], nb, LANE, n)
    t_in, w_st, w_out_t, c_t = _s5_build(kt.reshape(nb, LANE, q * gs), by_block(ba_re), by_block(ba_im),
                                         by_block(ca_re), by_block(ca_im))

    lanes = lambda p: p.reshape(-1, nb, S5_LBN).transpose(1, 0, 2)
    rows = jnp.arange(SUBLANE)[None, :, None]
    pr, pi = lanes(pq_re), lanes(pq_im)
    tabs = [pr, pi]
    for d in (1, 2, 4):
        for p in (pr, pi):
            tabs.append(jnp.where(rows >= d, p[:, d - 1:d, :], 0.0))
    tab = jnp.stack(tabs, axis=1)
    abar = jnp.stack([lanes(p1_re), lanes(p1_im)], axis=1)
    return (t_in, w_st, w_out_t, tab), (w_st, c_t, abar)


def _s5_scan_tile(xr, xi, tab_ref, car_re, car_im):
    for d, k in ((1, 2), (2, 4), (4, 6)):
        ar = tab_ref[k]
        ai = tab_ref[k + 1]
        sr = pltpu.roll(xr, d, 0)
        si = pltpu.roll(xi, d, 0)
        xr, xi = xr + ar * sr - ai * si, xi + ar * si + ai * sr
    pr = tab_ref[0]
    pi = tab_ref[1]
    hr = xr + pr * car_re - pi * car_im
    hi = xi + pr * car_im + pi * car_re
    return hr, hi


def _s5_prompt_body(u_ref, tin_ref, wst_ref, wout_ref, tab_ref, d_ref, y_ref, hfin_ref, car, uf, yf):
    q = S5_Q
    j = pl.program_id(2)
    nj = pl.num_programs(2)
    nrow = u_ref.shape[0] // q

    @pl.when(j == 0)
    def _():
        car[...] = jnp.zeros(car.shape, F32)

    uf[...] = u_ref[...].astype(F32)
    xb = jnp.concatenate([uf[pl.ds(t, nrow, stride=q), :] for t in range(q)], axis=1).astype(BF16)
    s = jnp.dot(xb, wst_ref[...], preferred_element_type=F32)
    car_re = car[:, 0:S5_LBN]
    car_im = car[:, S5_LBN:]
    first = lax.broadcasted_iota(jnp.int32, (SUBLANE, S5_LBN), 0) == 0
    hp = []
    for t in range(nrow // SUBLANE):
        rs = slice(t * SUBLANE, (t + 1) * SUBLANE)
        hr, hi = _s5_scan_tile(s[rs, 0:S5_LBN], s[rs, S5_LBN:], tab_ref, car_re, car_im)
        hp.append(jnp.concatenate([jnp.where(first, car_re, pltpu.roll(hr, 1, 0)),
                                   jnp.where(first, car_im, pltpu.roll(hi, 1, 0))], axis=1))
        car_re = jnp.broadcast_to(hr[SUBLANE - 1:SUBLANE, :], hr.shape)
        car_im = jnp.broadcast_to(hi[SUBLANE - 1:SUBLANE, :], hi.shape)
    car[:, 0:S5_LBN] = car_re
    car[:, S5_LBN:] = car_im
    h_in = jnp.concatenate(hp, axis=0).astype(BF16)
    d_row = jnp.concatenate([d_ref[...]] * q, axis=1)
    y = (jnp.dot(xb, tin_ref[...], preferred_element_type=F32)
         + lax.dot_general(h_in, wout_ref[...], (((1,), (1,)), ((), ())), preferred_element_type=F32)
         + d_row * xb.astype(F32))
    y = _gelu_tanh(y)
    for t in range(q):
        yf[pl.ds(t, nrow, stride=q), :] = y[:, t * LANE:(t + 1) * LANE]
    y_ref[...] = yf[...].astype(y_ref.dtype)

    @pl.when(j == nj - 1)
    def _():
        hfin_ref[...] = car[...]


def _s5_prompt(proj, t_in, w_st, w_out, tab, d_skip, ntok):
    b, l, cols = proj.shape
    q = S5_Q
    nb = S5_GROUPS // S5_LB
    ub = P_U // LANE
    wmap = lambda g, i, j: (g, 0, 0)
    return pl.pallas_call(
        _s5_prompt_body,
        grid=(nb, b, l // ntok),
        in_specs=[pl.BlockSpec((None, ntok, LANE), lambda g, i, j: (i, j, ub + g)),
                  pl.BlockSpec((None, q * LANE, q * LANE), wmap),
                  pl.BlockSpec((None, q * LANE, 2 * S5_LBN), wmap),
                  pl.BlockSpec((None, 2 * S5_LBN, q * LANE), wmap),
                  pl.BlockSpec((None, 8, SUBLANE, S5_LBN), lambda g, i, j: (g, 0, 0, 0)),
                  pl.BlockSpec((1, LANE), lambda g, i, j: (0, g))],
        out_specs=[pl.BlockSpec((None, None, ntok, LANE), lambda g, i, j: (g, i, j, 0)),
                   pl.BlockSpec((None, None, SUBLANE, 2 * S5_LBN), lambda g, i, j: (i, g, 0, 0))],
        out_shape=[_sds((nb, b, l, LANE), BF16), _sds((b, nb, SUBLANE, 2 * S5_LBN))],
        scratch_shapes=[pltpu.VMEM((SUBLANE, 2 * S5_LBN), F32), pltpu.VMEM((ntok, LANE), F32),
                        pltpu.VMEM((ntok, LANE), F32)],
        compiler_params=_cparams(3),
        name="s5_prompt",
    )(proj, t_in, w_st, w_out, tab, d_skip)


def _rope_tables(pos):
    half = ROT_DIM // 2
    inv_freq = jnp.exp(-(2.0 * jnp.arange(half, dtype=F32) / ROT_DIM) * math.log(ROPE_THETA))
    ang = pos.astype(F32)[:, None] * inv_freq[None, :]
    cos, sin = jnp.cos(ang), jnp.sin(ang)
    l = pos.shape[0]
    one = jnp.ones((l, HEAD_DIM - ROT_DIM), F32)
    zero = jnp.zeros((l, HEAD_DIM - ROT_DIM), F32)
    zh = jnp.zeros((l, half), F32)
    cos_h = jnp.concatenate([cos, cos, one], axis=1)
    sa_h = jnp.concatenate([-sin, zh, zero], axis=1)
    sb_h = jnp.concatenate([zh, sin, zero], axis=1)
    two = lambda t: jnp.concatenate([t, t], axis=1)
    return two(cos_h), two(sa_h), two(sb_h)


def _rope_flat(x, cos, sa, sb):
    w = x.shape[1]
    n = w // LANE
    tile = lambda t: jnp.concatenate([t] * n, axis=1) if n > 1 else t
    half = ROT_DIM // 2
    return x * tile(cos) + pltpu.roll(x, w - half, 1) * tile(sa) + pltpu.roll(x, half, 1) * tile(sb)


def _attn_prompt_body(q_ref, k_ref, v_ref, cos_ref, sa_ref, sb_ref, sink_ref, wo_ref, y_ref, kn_ref, vn_ref,
                      kprev, vprev):
    j = pl.program_id(1)
    nb = pl.num_programs(1)
    w = WINDOW
    kvw = KV_HEADS * HEAD_DIM

    @pl.when(j == 0)
    def _():
        kprev[...] = jnp.zeros(kprev.shape, F32)
        vprev[...] = jnp.zeros(vprev.shape, F32)

    cos, sa, sb = cos_ref[...], sa_ref[...], sb_ref[...]
    q = _rope_flat(q_ref[...].astype(F32), cos, sa, sb) * (HEAD_DIM ** -0.5)
    k = _rope_flat(k_ref[...].astype(F32), cos, sa, sb)
    v = v_ref[...].astype(F32)
    kctx = jnp.concatenate([kprev[...], k], axis=0).astype(BF16)
    vctx = jnp.concatenate([vprev[...], v], axis=0).astype(BF16)
    rows = Q_PER_KV * w
    ti = lax.broadcasted_iota(jnp.int32, (rows, 2 * w), 0) % w
    ci = lax.broadcasted_iota(jnp.int32, (rows, 2 * w), 1)
    valid = (ci >= ti) & (ci <= ti + w) & ((ci >= w) | (j > 0))
    ones = jnp.ones((2 * w, LANE), BF16)
    qb = q.astype(BF16)
    outs = []
    for kv in range(KV_HEADS):
        kj = kctx[:, kv * HEAD_DIM:(kv + 1) * HEAD_DIM]
        vj = vctx[:, kv * HEAD_DIM:(kv + 1) * HEAD_DIM]
        heads = range(kv * Q_PER_KV, (kv + 1) * Q_PER_KV)
        qs = jnp.concatenate([qb[:, h * HEAD_DIM:(h + 1) * HEAD_DIM] for h in heads], axis=0)
        sink = jnp.concatenate([jnp.broadcast_to(sink_ref[0:1, h:h + 1], (w, 1)) for h in heads], axis=0)
        s = lax.dot_general(qs, kj, (((1,), (1,)), ((), ())), preferred_element_type=F32)
        s = jnp.where(valid, s, -jnp.inf)
        m = jnp.maximum(jnp.max(s, axis=-1, keepdims=True), sink)
        p = jnp.exp(s - m).astype(BF16)
        den = jnp.dot(p, ones, preferred_element_type=F32)[:, 0:HEAD_DIM] + jnp.exp(sink - m)
        o = jnp.dot(p, vj, preferred_element_type=F32) / den
        outs.extend(o[i * w:(i + 1) * w] for i in range(Q_PER_KV))
    o = jnp.concatenate(outs, axis=1)
    y_ref[...] = jnp.dot(o.astype(BF16), wo_ref[...], preferred_element_type=F32).astype(y_ref.dtype)
    kprev[...] = k
    vprev[...] = v

    @pl.when(j == nb - 1)
    def _():
        kn_ref[...] = k
        vn_ref[...] = v


def _attn_prompt(proj, cos, sa, sb, sinks, attn_o):
    b, l, _ = proj.shape
    d = D_MODEL
    w = WINDOW
    kvw = KV_HEADS * HEAD_DIM
    const2 = lambda i, j: (0, 0)
    tmap = lambda i, j: (j, 0)
    return pl.pallas_call(
        _attn_prompt_body,
        grid=(b, l // w),
        in_specs=[pl.BlockSpec((None, w, d), lambda i, j: (i, j, P_Q // d)),
                  pl.BlockSpec((None, w, kvw), lambda i, j: (i, j, P_K // kvw)),
                  pl.BlockSpec((None, w, kvw), lambda i, j: (i, j, P_V // kvw)),
                  pl.BlockSpec((w, LANE), tmap), pl.BlockSpec((w, LANE), tmap), pl.BlockSpec((w, LANE), tmap),
                  pl.BlockSpec((1, LANE), const2),
                  pl.BlockSpec((d, d), const2)],
        out_specs=[pl.BlockSpec((None, w, d), lambda i, j: (i, j, 0)),
                   pl.BlockSpec((None, w, kvw), lambda i, j: (i, 0, 0)),
                   pl.BlockSpec((None, w, kvw), lambda i, j: (i, 0, 0))],
        out_shape=[_sds((b, l, d), BF16), _sds((b, w, kvw)), _sds((b, w, kvw))],
        scratch_shapes=[pltpu.VMEM((w, kvw), F32), pltpu.VMEM((w, kvw), F32)],
        compiler_params=_cparams(2),
        name="attn_prompt",
    )(proj, proj, proj, cos, sa, sb, sinks, attn_o)


SAMPLE_TB = 8


def _mamba_sample_body(x_ref, bc_ref, dt_ref, cst_ref, ssm_ref, cw_ref, cb_ref, dtb_ref, alog_ref, dskip_ref,
                       exp_ref, stack_ref, y_ref, cnew_ref, ssmo_ref):
    del stack_ref
    tb = SAMPLE_TB
    new = jnp.concatenate([x_ref[...], bc_ref[...]], axis=1)
    conv = (cb_ref[...] + cw_ref[0:1, :] * cst_ref[0] + cw_ref[1:2, :] * cst_ref[1] + cw_ref[2:3, :] * cst_ref[2]
            + cw_ref[3:4, :] * new)
    cnew_ref[0] = cst_ref[1]
    cnew_ref[1] = cst_ref[2]
    cnew_ref[2] = new
    xbc = _silu(conv)
    xs = xbc[:, 0:M_INNER]
    gn = M_GROUPS * M_STATE
    bm = xbc[:, M_INNER:M_INNER + gn]
    cm = xbc[:, M_INNER + gn:].astype(BF16)
    dt = _softplus(dt_ref[...] + dtb_ref[...])
    a = dt * (-jnp.exp(alog_ref[...]))
    expand = exp_ref[...]
    dtx = _dot3_right(dt, expand) * xs
    da_x = jnp.exp(_dot3_right(a, expand))
    nl = M_INNER // LANE
    slab = jnp.concatenate([dtx[:, j * LANE:(j + 1) * LANE] for j in range(nl)]
                           + [da_x[:, j * LANE:(j + 1) * LANE] for j in range(nl)], axis=0)
    tt = slab.T
    hpg = M_HEADS // M_GROUPS
    gw = hpg * M_HEADDIM
    rowid = lax.broadcasted_iota(jnp.int32, (tb, gw), 0)
    ys = []
    for g in range(M_GROUPS):
        yacc = jnp.zeros((tb, gw), F32)
        for t in range(tb):
            parts = []
            for hl in range(hpg):
                h = g * hpg + hl
                j, hh = divmod(h * M_HEADDIM, LANE)
                col = j * tb + t
                dcol = tt[hh:hh + M_HEADDIM, col:col + 1]
                acol = tt[hh:hh + M_HEADDIM, nl * tb + col:nl * tb + col + 1]
                hn = acol * ssm_ref[t, h] + dcol * bm[t:t + 1, g * M_STATE:(g + 1) * M_STATE]
                ssmo_ref[t, h] = hn
                parts.append(hn)
            hng = jnp.concatenate(parts, axis=0).astype(BF16)
            yg = lax.dot_general(cm[:, g * M_STATE:(g + 1) * M_STATE], hng, (((1,), (1,)), ((), ())),
                                 preferred_element_type=F32)
            yacc = jnp.where(rowid == t, yg, yacc)
        ys.append(yacc)
    y_ref[...] = jnp.concatenate(ys, axis=1) + xs * dskip_ref[...]


def _mamba_sample(proj, dtp, conv_t, ssm, conv_w, conv_b, dt_bias, a_log, d_skip, expand, ssm_stack, li):
    s = proj.shape[0]
    d = D_MODEL
    tb = SAMPLE_TB
    const = lambda i: (0, 0)
    return pl.pallas_call(
        _mamba_sample_body,
        grid=(s // tb,),
        in_specs=[pl.BlockSpec((tb, d), lambda i: (i, S_X // d)),
                  pl.BlockSpec((tb, d), lambda i: (i, S_BC // d)),
                  pl.BlockSpec((tb, LANE), lambda i: (i, 0)),
                  pl.BlockSpec((CONV_W - 1, tb, CONV_DIM), lambda i: (0, i, 0)),
                  pl.BlockSpec((None, tb, M_HEADS, M_HEADDIM, M_STATE), lambda i: (li, i, 0, 0, 0)),
                  pl.BlockSpec((CONV_W, CONV_DIM), const), pl.BlockSpec((1, CONV_DIM), const),
                  pl.BlockSpec((1, LANE), const), pl.BlockSpec((1, LANE), const), pl.BlockSpec((1, d), const),
                  pl.BlockSpec((LANE, d), const),
                  pl.BlockSpec(memory_space=pl.ANY)],
        out_specs=[pl.BlockSpec((tb, d), lambda i: (i, 0)),
                   pl.BlockSpec((CONV_W - 1, tb, CONV_DIM), lambda i: (0, i, 0)),
                   pl.BlockSpec((None, tb, M_HEADS, M_HEADDIM, M_STATE), lambda i: (li, i, 0, 0, 0))],
        out_shape=[_sds((s, d)), _sds((CONV_W - 1, s, CONV_DIM)), _sds(ssm_stack.shape)],
        input_output_aliases={11: 2},
        compiler_params=_cparams(1),
        name="mamba_sample",
    )(proj, proj, dtp, conv_t, ssm, conv_w, conv_b, dt_bias, a_log, d_skip, expand, ssm_stack)


def _gate_norm_proj_body(y_ref, z_ref, nw_ref, w_ref, o_ref):
    y = _group_norm_gate(y_ref[...], z_ref[...], nw_ref[...])
    o_ref[...] = jnp.dot(y.astype(BF16), w_ref[...], preferred_element_type=F32)


def _gate_norm_proj(y, proj, m_norm_w, m_proj):
    s, d = y.shape
    const = lambda i: (0, 0)
    return pl.pallas_call(
        _gate_norm_proj_body,
        grid=(1,),
        in_specs=[pl.BlockSpec((s, d), const), pl.BlockSpec((s, d), lambda i: (0, S_Z // d)),
                  pl.BlockSpec((1, d), const), pl.BlockSpec((d, d), const)],
        out_specs=pl.BlockSpec((s, d), const),
        out_shape=_sds((s, d)),
        compiler_params=_cparams(1),
        name="gate_norm_proj",
    )(y, proj, m_norm_w, m_proj)


def _s5_sample_body(u_ref, hre_ref, him_ref, wb_ref, ct_ref, tab_ref, d_ref, y_ref, ore_ref, oim_ref):
    u = u_ref[...]
    bu = jnp.dot(u.astype(BF16), wb_ref[...], preferred_element_type=F32)
    bre = bu[:, 0:S5_LBN]
    bim = bu[:, S5_LBN:]
    ar = tab_ref[0]
    ai = tab_ref[1]
    h0r = hre_ref[...]
    h0i = him_ref[...]
    hr = ar * h0r - ai * h0i + bre
    hi = ar * h0i + ai * h0r + bim
    ore_ref[...] = hr
    oim_ref[...] = hi
    h = jnp.concatenate([hr, hi], axis=1).astype(BF16)
    y = lax.dot_general(h, ct_ref[...], (((1,), (1,)), ((), ())), preferred_element_type=F32) + d_ref[...] * u
    y_ref[...] = _gelu_tanh(y)


def _s5_sample(proj, h_re, h_im, w_st, c_t, tab, d_skip):
    s = proj.shape[0]
    nb = S5_GROUPS // S5_LB
    ub = S_U // LANE
    wmap = lambda g: (g, 0, 0)
    hspec = pl.BlockSpec((s, S5_LBN), lambda g: (0, g))
    return pl.pallas_call(
        _s5_sample_body,
        grid=(nb,),
        in_specs=[pl.BlockSpec((s, LANE), lambda g: (0, ub + g)), hspec, hspec,
                  pl.BlockSpec((None, LANE, 2 * S5_LBN), lambda g: (g, S5_Q - 1, 0)),
                  pl.BlockSpec((None, LANE, 2 * S5_LBN), wmap),
                  pl.BlockSpec((None, 2, 1, S5_LBN), lambda g: (g, 0, 0, 0)),
                  pl.BlockSpec((1, LANE), lambda g: (0, g))],
        out_specs=[pl.BlockSpec((None, s, LANE), wmap), hspec, hspec],
        out_shape=[_sds((nb, s, LANE)), _sds(h_re.shape), _sds(h_im.shape)],
        compiler_params=_cparams(1),
        name="s5_sample",
    )(proj, h_re, h_im, w_st, c_t, tab, d_skip)


def _rope_sample_body(q_ref, k_ref, cos_ref, sa_ref, sb_ref, qo_ref, ko_ref):
    cos, sa, sb = cos_ref[...], sa_ref[...], sb_ref[...]
    qo_ref[...] = _rope_flat(q_ref[...], cos, sa, sb) * (HEAD_DIM ** -0.5)
    ko_ref[...] = _rope_flat(k_ref[...], cos, sa, sb)


def _rope_sample(proj, cos, sa, sb):
    s = proj.shape[0]
    kvw = KV_HEADS * HEAD_DIM
    const = lambda i: (0, 0)
    return pl.pallas_call(
        _rope_sample_body,
        grid=(1,),
        in_specs=[pl.BlockSpec((s, QX), lambda i: (0, S_QX // QX)), pl.BlockSpec((s, kvw), lambda i: (0, S_K // kvw)),
                  pl.BlockSpec((1, LANE), const), pl.BlockSpec((1, LANE), const), pl.BlockSpec((1, LANE), const)],
        out_specs=[pl.BlockSpec((s, QX), const), pl.BlockSpec((s, kvw), const)],
        out_shape=[_sds((s, QX)), _sds((s, kvw))],
        compiler_params=_cparams(1),
        name="rope_sample",
    )(proj, proj, cos, sa, sb)


def _attn_sample_body(q_ref, kc_ref, vc_ref, kn_ref, vn_ref, sink_ref, kstack_ref, vstack_ref, o_ref, ko_ref, vo_ref):
    del kstack_ref, vstack_ref
    q = q_ref[...]
    kc = kc_ref[...]
    vc = vc_ref[...]
    kn = kn_ref[...]
    vn = vn_ref[...]
    w = kc.shape[1]
    s_c = lax.dot_general(q.astype(BF16), kc.astype(BF16), (((2,), (2,)), ((0,), (0,))),
                          preferred_element_type=F32)
    s_n = jnp.sum(q * kn, axis=-1, keepdims=True)
    sink = sink_ref[...][None]
    m = jnp.maximum(jnp.maximum(jnp.max(s_c, axis=-1, keepdims=True), s_n), sink)
    p_c = jnp.exp(s_c - m)
    p_n = jnp.exp(s_n - m)
    den = jnp.sum(p_c, axis=-1, keepdims=True) + p_n + jnp.exp(sink - m)
    o = lax.dot_general((p_c / den).astype(BF16), vc.astype(BF16), (((2,), (1,)), ((0,), (0,))),
                        preferred_element_type=F32)
    o_ref[...] = o + (p_n / den) * vn
    ko_ref[:, 0:w - 1, :] = kc[:, 1:w, :]
    ko_ref[:, w - 1:w, :] = kn
    vo_ref[:, 0:w - 1, :] = vc[:, 1:w, :]
    vo_ref[:, w - 1:w, :] = vn


def _attn_sample(qx, kc, vc, kn, proj3, sinks, k_stack, v_stack, li):
    s, w, kvw = kc.shape
    tb = SAMPLE_TB
    row3 = lambda i: (i, 0, 0)
    slab = pl.BlockSpec((None, tb, w, kvw), lambda i: (li, i, 0, 0))
    return pl.pallas_call(
        _attn_sample_body,
        grid=(s // tb,),
        in_specs=[pl.BlockSpec((tb, A_HEADS, kvw), row3), pl.BlockSpec((tb, w, kvw), row3),
                  pl.BlockSpec((tb, w, kvw), row3), pl.BlockSpec((tb, 1, kvw), row3),
                  pl.BlockSpec((tb, 1, kvw), lambda i: (i, 0, S_V // kvw)),
                  pl.BlockSpec((A_HEADS, 1), lambda i: (0, 0)),
                  pl.BlockSpec(memory_space=pl.ANY), pl.BlockSpec(memory_space=pl.ANY)],
        out_specs=[pl.BlockSpec((tb, A_HEADS, kvw), row3), slab, slab],
        out_shape=[_sds((s, A_HEADS, kvw)), _sds(k_stack.shape), _sds(v_stack.shape)],
        input_output_aliases={6: 1, 7: 2},
        compiler_params=_cparams(1),
        name="attn_sample",
    )(qx, kc, vc, kn, proj3, sinks, k_stack, v_stack)


def _dense_body(x_ref, w_ref, o_ref):
    o_ref[...] = jnp.dot(x_ref[...].astype(BF16), w_ref[...], preferred_element_type=F32)


def _dense(x, w, tn):
    m, k = x.shape
    n = w.shape[1]
    return pl.pallas_call(
        _dense_body,
        grid=(n // tn,),
        in_specs=[pl.BlockSpec((m, k), lambda j: (0, 0)), pl.BlockSpec((k, tn), lambda j: (0, j))],
        out_specs=pl.BlockSpec((m, tn), lambda j: (0, j)),
        out_shape=_sds((m, n)),
        compiler_params=_cparams(1),
        name="dense",
    )(x, w)


def _pad_lanes(v):
    return jnp.zeros((1, LANE), F32).at[0, :v.shape[0]].set(v.astype(F32))


def _layer_weights(l, w_in, attn_o):
    w = w_in[l]
    seg = lambda a, b: w[:, a:b]
    z, xbc, dt, u = seg(OFF_Z, OFF_XBC), seg(OFF_XBC, OFF_DT), seg(OFF_DT, OFF_U), seg(OFF_U, OFF_Q)
    q, k, v, g = seg(OFF_Q, OFF_K), seg(OFF_K, OFF_V), seg(OFF_V, OFF_G), seg(OFF_G, IN_COLS)
    w_prompt = jnp.concatenate([z, xbc, u, q, g, k, v], axis=1).astype(BF16)
    sel = (jnp.arange(A_HEADS)[:, None] // Q_PER_KV == jnp.arange(KV_HEADS)[None, :]).astype(F32)
    qx = (q.reshape(D_MODEL, A_HEADS, 1, HEAD_DIM) * sel[None, :, :, None]).reshape(D_MODEL, QX)
    w_sample = jnp.concatenate([z, xbc, u, qx, g, k, v], axis=1).astype(BF16)
    w_dt = jnp.zeros((D_MODEL, LANE), F32).at[:, :M_HEADS].set(dt).astype(BF16)
    ao = attn_o[l].reshape(A_HEADS, 1, HEAD_DIM, D_MODEL) * sel[:, :, None, None]
    return w_prompt, w_sample, w_dt, ao.reshape(QX, D_MODEL).astype(BF16)


def kernel(x_prompt, x_sample, state_ssm, state_conv, state_s5_re, state_s5_im, cache_k, cache_v, norm1_w, w_in,
           conv_w, conv_b, dt_bias, a_log, m_d, m_norm_w, m_proj, s5_lam_re, s5_lam_im, s5_log_step, s5_b_re,
           s5_b_im, s5_c_re, s5_c_im, s5_d, s5_glu_w, attn_sinks, attn_o, w_out, norm2_w, mlp_up, mlp_down,
           final_norm_w):
    b, l, d = x_prompt.shape
    s = x_sample.shape[0]
    kvw = KV_HEADS * HEAD_DIM
    nb = S5_GROUPS // S5_LB
    xp = x_prompt.reshape(b * l, d)
    xs = x_sample.reshape(s, d)
    cos_p, sa_p, sb_p = _rope_tables(jnp.arange(l, dtype=jnp.int32))
    cos_s, sa_s, sb_s = _rope_tables(jnp.full((1,), PAST_LEN, jnp.int32))
    expand = (jnp.arange(LANE)[:, None] == jnp.arange(M_INNER)[None, :] // M_HEADDIM).astype(BF16)
    fnw = final_norm_w[None]
    ssm_s = jnp.zeros((DEPTH,) + state_ssm.shape[1:], F32)
    k_s = jnp.zeros((DEPTH, s, WINDOW, kvw), F32)
    v_s = jnp.zeros((DEPTH, s, WINDOW, kvw), F32)
    outs = [[] for _ in range(9)]
    for li in range(DEPTH):
        final = li == DEPTH - 1
        w_prompt, w_sample, w_dt, ao_x = _layer_weights(li, w_in, attn_o)
        n1 = norm1_w[li][None]
        cw, cb = conv_w[li], conv_b[li][None]
        dtb, alog = _pad_lanes(dt_bias[li]), _pad_lanes(a_log[li])
        dsk = jnp.repeat(m_d[li], M_HEADDIM)[None]
        mnw = m_norm_w[li][None]
        mpj = m_proj[li].astype(BF16)
        s5_chunked, s5_step = _s5_operators(s5_lam_re[li], s5_lam_im[li], s5_log_step[li], s5_b_re[li], s5_b_im[li],
                                            s5_c_re[li], s5_c_im[li])
        s5d = s5_d[li][None]
        sinks = attn_sinks[li]
        ao = attn_o[li].astype(BF16)
        mlp_w = (s5_glu_w[li].astype(BF16), w_out[li].astype(BF16), norm2_w[li][None], mlp_up[li].astype(BF16),
                 mlp_down[li].astype(BF16), fnw)

        proj, dtp = _norm_proj(xp, n1, w_prompt, w_dt, 2048, 512, BF16)
        proj3 = proj.reshape(b, l, P_COLS)
        ym, conv_p, ssm_p = _mamba_prompt(proj3, dtp.reshape(b, l, LANE), cw, cb, dtb, alog, dsk, mnw, expand, mpj)
        ys, h_p = _s5_prompt(proj3, *s5_chunked, s5d, 2048)
        hre_p, him_p = h_p[:, :, 0, :S5_LBN], h_p[:, :, 0, S5_LBN:]
        ya, k_p, v_p = _attn_prompt(proj3, cos_p, sa_p, sb_p, _pad_lanes(sinks), ao)
        xp = _merge_mlp(xp, proj, P_G, ym.reshape(b * l, d), ys.reshape(nb, b * l, LANE), ya.reshape(b * l, d),
                        *mlp_w, 512, final)

        sproj, sdt = _norm_proj(xs, n1, w_sample, w_dt, s, 512, F32)
        y_pre, conv_t, ssm_s = _mamba_sample(sproj, sdt, state_conv[li].transpose(1, 0, 2), state_ssm, cw, cb,
                                             dtb, alog, dsk, expand, ssm_s, li)
        sym = _gate_norm_proj(y_pre, sproj, mnw, mpj)
        sys_, hre_s, him_s = _s5_sample(sproj, state_s5_re[li].reshape(s, S5_GROUPS * S5_STATE),
                                        state_s5_im[li].reshape(s, S5_GROUPS * S5_STATE), *s5_step, s5d)
        qx_rot, k_rot = _rope_sample(sproj, cos_s, sa_s, sb_s)
        o, k_s, v_s = _attn_sample(qx_rot.reshape(s, A_HEADS, kvw), cache_k[li].reshape(s, WINDOW, kvw),
                                   cache_v[li].reshape(s, WINDOW, kvw), k_rot.reshape(s, 1, kvw),
                                   sproj.reshape(s, 1, S_COLS), sinks[:, None], k_s, v_s, li)
        sya = _dense(o.reshape(s, QX), ao_x, 512)
        xs = _merge_mlp(xs, sproj, S_G, sym, sys_, sya, *mlp_w, s, final)

        for i, val in enumerate((
                ssm_p, conv_p, conv_t.transpose(1, 0, 2),
                hre_p.reshape(b, S5_GROUPS, S5_STATE), hre_s.reshape(s, S5_GROUPS, S5_STATE),
                him_p.reshape(b, S5_GROUPS, S5_STATE), him_s.reshape(s, S5_GROUPS, S5_STATE),
                k_p.reshape(b, WINDOW, KV_HEADS, HEAD_DIM), v_p.reshape(b, WINDOW, KV_HEADS, HEAD_DIM))):
            outs[i].append(val)
    ssm_p, conv_p, conv_s, hre_p, hre_s, him_p, him_s, k_p, v_p = (jnp.stack(o) for o in outs)
    cache_shape = (DEPTH, s, WINDOW, KV_HEADS, HEAD_DIM)
    return (xp.reshape(b, l, d), xs.reshape(s, 1, d), ssm_p, ssm_s, conv_p, conv_s, hre_p, hre_s, him_p, him_s,
            k_p, k_s.reshape(cache_shape), v_p, v_s.reshape(cache_shape))
```

```python
import functools
import math

import jax
import jax.numpy as jnp
from jax import lax
from jax.experimental import pallas as pl
from jax.experimental.pallas import tpu as pltpu

F32 = jnp.float32
BF16 = jnp.bfloat16

D_MODEL = 1024
DEPTH = 4
PAST_LEN = 8192
M_HEADDIM = 64
M_HEADS = 16
M_INNER = 1024
M_GROUPS = 4
M_STATE = 128
CONV_W = 4
CONV_DIM = 2048
CHUNK = 128
S5_GROUPS = 64
S5_GSIZE = 16
S5_STATE = 64
HEAD_DIM = 64
A_HEADS = 16
KV_HEADS = 4
Q_PER_KV = 4
ROT_DIM = 16
ROPE_THETA = 500000.0
WINDOW = 128
D_FF = 4096
EPS = 1e-6

OFF_Z, OFF_XBC, OFF_DT, OFF_U, OFF_Q, OFF_K, OFF_V, OFF_G, IN_COLS = 0, 1024, 3072, 3088, 4112, 5136, 5392, 5648, 8720

LANE = 128
SUBLANE = 8
S5_LB = 8
S5_LBN = S5_LB * S5_STATE
VMEM_LIMIT = 56 * 1024 * 1024

P_Z, P_X, P_BC, P_U, P_Q, P_G, P_K, P_V, P_COLS = 0, 1024, 2048, 3072, 4096, 5120, 8192, 8448, 8704
S_Z, S_X, S_BC, S_U, S_QX, S_G, S_K, S_V, S_COLS = 0, 1024, 2048, 3072, 4096, 8192, 11264, 11520, 11776
QX = A_HEADS * KV_HEADS * HEAD_DIM


def _sds(shape, dtype=F32):
    return jax.ShapeDtypeStruct(shape, dtype)


def _cparams(n_axes):
    return pltpu.CompilerParams(dimension_semantics=("arbitrary",) * n_axes, vmem_limit_bytes=VMEM_LIMIT)


def _sigmoid(x):
    return 0.5 * (1.0 + jnp.tanh(0.5 * x))


def _silu(x):
    return x * _sigmoid(x)


def _softplus(x):
    return jnp.maximum(x, 0.0) + jnp.log(1.0 + jnp.exp(-jnp.abs(x)))


def _gelu_tanh(x):
    return 0.5 * x * (1.0 + jnp.tanh(math.sqrt(2.0 / math.pi) * (x + 0.044715 * (x * x * x))))


def _bdot(a, b):
    return jnp.dot(a.astype(BF16), b.astype(BF16), preferred_element_type=F32)


def _split3(a):
    hi = a.astype(BF16)
    r = a - hi.astype(F32)
    mid = r.astype(BF16)
    lo = (r - mid.astype(F32)).astype(BF16)
    return hi, mid, lo


def _dot3_right(a, sel):
    hi, mid, lo = _split3(a)
    return (jnp.dot(hi, sel, preferred_element_type=F32) + jnp.dot(mid, sel, preferred_element_type=F32)
            + jnp.dot(lo, sel, preferred_element_type=F32))


def _dot3_left(sel, a):
    hi, mid, lo = _split3(a)
    return (jnp.dot(sel, hi, preferred_element_type=F32) + jnp.dot(sel, mid, preferred_element_type=F32)
            + jnp.dot(sel, lo, preferred_element_type=F32))


def _rms(x, w):
    return x * lax.rsqrt(jnp.mean(x * x, axis=-1, keepdims=True) + EPS) * w


def _norm_proj_body(x_ref, nw_ref, w_ref, wdt_ref, o_ref, odt_ref, h_scr):
    @pl.when(pl.program_id(1) == 0)
    def _():
        hb = _rms(x_ref[...], nw_ref[...]).astype(BF16)
        h_scr[...] = hb
        odt_ref[...] = jnp.dot(hb, wdt_ref[...], preferred_element_type=F32)

    o_ref[...] = jnp.dot(h_scr[...], w_ref[...], preferred_element_type=F32).astype(o_ref.dtype)


def _norm_proj(x, nw, w, wdt, tm, tn, out_dtype):
    m, d = x.shape
    n = w.shape[1]
    return pl.pallas_call(
        _norm_proj_body,
        grid=(m // tm, n // tn),
        in_specs=[pl.BlockSpec((tm, d), lambda i, j: (i, 0)),
                  pl.BlockSpec((1, d), lambda i, j: (0, 0)),
                  pl.BlockSpec((d, tn), lambda i, j: (0, j)),
                  pl.BlockSpec((d, LANE), lambda i, j: (0, 0))],
        out_specs=[pl.BlockSpec((tm, tn), lambda i, j: (i, j)),
                   pl.BlockSpec((tm, LANE), lambda i, j: (i, 0))],
        out_shape=[_sds((m, n), out_dtype), _sds((m, LANE))],
        scratch_shapes=[pltpu.VMEM((tm, d), BF16)],
        compiler_params=_cparams(2),
        name="norm_proj",
    )(x, nw, w, wdt)


def _merge_mlp_body(final, x_ref, g0_ref, g1_ref, g2_ref, ym_ref, ys_ref, ya_ref, glu_ref, wo_ref, n2_ref, up_ref,
                    dn_ref, fn_ref, o_ref):
    ys = jnp.concatenate([ys_ref[g] for g in range(ys_ref.shape[0])], axis=1)
    glu = jnp.dot(ys.astype(BF16), glu_ref[...], preferred_element_type=F32)
    y_s = glu[:, :D_MODEL] * _sigmoid(glu[:, D_MODEL:])
    f32 = lambda r: r[...].astype(F32)
    merged = _sigmoid(f32(g0_ref)) * f32(ym_ref) + _sigmoid(f32(g1_ref)) * y_s + _sigmoid(f32(g2_ref)) * f32(ya_ref)
    x1 = x_ref[...] + jnp.dot(merged.astype(BF16), wo_ref[...], preferred_element_type=F32)
    h2 = _rms(x1, n2_ref[...]).astype(BF16)
    acc = x1
    fc = 1024
    for c in range(D_FF // fc):
        a = jnp.dot(h2, up_ref[:, c * fc:(c + 1) * fc], preferred_element_type=F32)
        a = jnp.square(jnp.maximum(a, 0.0))
        acc = acc + jnp.dot(a.astype(BF16), dn_ref[c * fc:(c + 1) * fc, :], preferred_element_type=F32)
    o_ref[...] = _rms(acc, fn_ref[...]) if final else acc


def _merge_mlp(x, proj, g_col, ym, ys, ya, glu_w, w_out, n2, up, dn, fnw, tm, final):
    m, d = x.shape
    gb = g_col // d
    nlb = d // LANE
    row = lambda i: (i, 0)
    const = lambda i: (0, 0)
    wspec = lambda shape: pl.BlockSpec(shape, const, pipeline_mode=pl.Buffered(1))
    return pl.pallas_call(
        functools.partial(_merge_mlp_body, final),
        grid=(m // tm,),
        in_specs=[pl.BlockSpec((tm, d), row),
                  pl.BlockSpec((tm, d), lambda i: (i, gb)),
                  pl.BlockSpec((tm, d), lambda i: (i, gb + 1)),
                  pl.BlockSpec((tm, d), lambda i: (i, gb + 2)),
                  pl.BlockSpec((tm, d), row), pl.BlockSpec((nlb, tm, LANE), lambda i: (0, i, 0)),
                  pl.BlockSpec((tm, d), row),
                  wspec((d, 2 * d)), wspec((d, d)), wspec((1, d)), wspec((d, D_FF)), wspec((D_FF, d)),
                  wspec((1, d))],
        out_specs=pl.BlockSpec((tm, d), row),
        out_shape=_sds((m, d)),
        compiler_params=_cparams(1),
        name="merge_mlp",
    )(x, proj, proj, proj, ym, ys, ya, glu_w, w_out, n2, up, dn, fnw)


def _group_norm_gate(y, z, nw):
    y = y * _silu(z)
    gw = M_INNER // M_GROUPS
    parts = []
    for g in range(M_GROUPS):
        yg = y[:, g * gw:(g + 1) * gw]
        parts.append(yg * lax.rsqrt(jnp.mean(yg * yg, axis=-1, keepdims=True) + EPS))
    return jnp.concatenate(parts, axis=1) * nw


def _mamba_prompt_body(z_ref, x_ref, bc_ref, dt_ref, cw_ref, cb_ref, dtb_ref, alog_ref, dskip_ref, nw_ref,
                       exp_ref, mproj_ref, y_ref, conv_ref, ssm_ref, xprev, ht):
    c = pl.program_id(1)
    nc = pl.num_programs(1)
    q = CHUNK
    tail = xprev.shape[0]

    @pl.when(c == 0)
    def _():
        xprev[...] = jnp.zeros(xprev.shape, xprev.dtype)
        ht[...] = jnp.zeros(ht.shape, F32)

    xcur = jnp.concatenate([x_ref[...], bc_ref[...]], axis=1)
    xext = jnp.concatenate([xprev[...], xcur], axis=0)
    ri = lax.broadcasted_iota(jnp.int32, (q, tail + q), 0)
    ci = lax.broadcasted_iota(jnp.int32, (q, tail + q), 1)
    conv = cb_ref[...] + cw_ref[CONV_W - 1:CONV_W, :] * xcur.astype(F32)
    for k in range(CONV_W - 1):
        shift = (ci == ri + (tail - (CONV_W - 1) + k)).astype(BF16)
        conv = conv + cw_ref[k:k + 1, :] * jnp.dot(shift, xext, preferred_element_type=F32)
    last_rows = xcur[q - tail:q, :]
    xprev[...] = last_rows

    @pl.when(c == nc - 1)
    def _():
        conv_ref[...] = last_rows.astype(F32)[tail - (CONV_W - 1):tail, :]

    xbc = _silu(conv)
    xs = xbc[:, 0:M_INNER]
    gn = M_GROUPS * M_STATE
    bm = xbc[:, M_INNER:M_INNER + gn]
    cm = xbc[:, M_INNER + gn:]

    dt = _softplus(dt_ref[...] + dtb_ref[...])
    a = dt * (-jnp.exp(alog_ref[...]))
    ri = lax.broadcasted_iota(jnp.int32, (q, q), 0)
    ci = lax.broadcasted_iota(jnp.int32, (q, q), 1)
    causal = ci <= ri
    tri = causal.astype(BF16)
    acum = _dot3_left(tri, a)
    expand = exp_ref[...]
    acum_x = _dot3_right(acum, expand)
    dt_x = _dot3_right(dt, expand)
    acum_last = acum_x[q - 1:q, :]
    xdt = xs * dt_x
    xw = xs * (jnp.exp(acum_last - acum_x) * dt_x)
    e_acum = jnp.exp(acum_x)
    chunk_decay = jnp.exp(acum_last)
    acum_t = acum.T

    hpg = M_HEADS // M_GROUPS
    gw = hpg * M_HEADDIM
    y_parts = []
    for g in range(M_GROUPS):
        cg = cm[:, g * M_STATE:(g + 1) * M_STATE].astype(BF16)
        bg = bm[:, g * M_STATE:(g + 1) * M_STATE]
        bgb = bg.astype(BF16)
        cb = lax.dot_general(cg, bgb, (((1,), (1,)), ((), ())), preferred_element_type=F32)
        h_prev = ht[g]
        y_off = jnp.dot(cg, h_prev.astype(BF16), preferred_element_type=F32) * e_acum[:, g * gw:(g + 1) * gw]
        yd = []
        for hl in range(hpg):
            h = g * hpg + hl
            seg = acum[:, h:h + 1] - acum_t[h:h + 1, :]
            lmat = jnp.exp(jnp.where(causal, seg, -jnp.inf))
            mh = (cb * lmat).astype(BF16)
            yd.append(jnp.dot(mh, xdt[:, h * M_HEADDIM:(h + 1) * M_HEADDIM].astype(BF16),
                              preferred_element_type=F32))
        y_parts.append(jnp.concatenate(yd, axis=1) + y_off)
        s_new = jnp.dot(bg.T.astype(BF16), xw[:, g * gw:(g + 1) * gw].astype(BF16), preferred_element_type=F32)
        ht[g] = h_prev * chunk_decay[:, g * gw:(g + 1) * gw] + s_new
    y = jnp.concatenate(y_parts, axis=1) + xs * dskip_ref[...]
    y = _group_norm_gate(y, z_ref[...].astype(F32), nw_ref[...])
    y_ref[...] = jnp.dot(y.astype(BF16), mproj_ref[...], preferred_element_type=F32).astype(y_ref.dtype)

    @pl.when(c == nc - 1)
    def _():
        for g in range(M_GROUPS):
            ssm_ref[g * hpg:(g + 1) * hpg] = ht[g].T.reshape(hpg, M_HEADDIM, M_STATE)


def _mamba_prompt(proj, dtp, conv_w, conv_b, dt_bias, a_log, d_skip, m_norm_w, expand, m_proj):
    b, l, _ = proj.shape
    d = D_MODEL
    nc = l // CHUNK
    const2 = lambda i, j: (0, 0)
    return pl.pallas_call(
        _mamba_prompt_body,
        grid=(b, nc),
        in_specs=[pl.BlockSpec((None, CHUNK, d), lambda i, j: (i, j, P_Z // d)),
                  pl.BlockSpec((None, CHUNK, d), lambda i, j: (i, j, P_X // d)),
                  pl.BlockSpec((None, CHUNK, d), lambda i, j: (i, j, P_BC // d)),
                  pl.BlockSpec((None, CHUNK, LANE), lambda i, j: (i, j, 0)),
                  pl.BlockSpec((CONV_W, CONV_DIM), const2),
                  pl.BlockSpec((1, CONV_DIM), const2),
                  pl.BlockSpec((1, LANE), const2),
                  pl.BlockSpec((1, LANE), const2),
                  pl.BlockSpec((1, d), const2),
                  pl.BlockSpec((1, d), const2),
                  pl.BlockSpec((LANE, d), const2),
                  pl.BlockSpec((d, d), const2)],
        out_specs=[pl.BlockSpec((None, CHUNK, d), lambda i, j: (i, j, 0)),
                   pl.BlockSpec((None, CONV_W - 1, CONV_DIM), lambda i, j: (i, 0, 0)),
                   pl.BlockSpec((None, M_HEADS, M_HEADDIM, M_STATE), lambda i, j: (i, 0, 0, 0))],
        out_shape=[_sds((b, l, d), BF16), _sds((b, CONV_W - 1, CONV_DIM)), _sds((b, M_HEADS, M_HEADDIM, M_STATE))],
        scratch_shapes=[pltpu.VMEM((2 * SUBLANE, CONV_DIM), BF16),
                        pltpu.VMEM((M_GROUPS, M_STATE, (M_HEADS // M_GROUPS) * M_HEADDIM), F32)],
        compiler_params=_cparams(2),
        name="mamba_prompt",
    )(proj, proj, proj, dtp, conv_w, conv_b, dt_bias, a_log, d_skip, m_norm_w, expand, m_proj)


S5_Q = 8


def _cmul(a, b):
    return a[0] * b[0] - a[1] * b[1], a[0] * b[1] + a[1] * b[0]


def _s5_disc_body(lr_ref, li_ref, ls_ref, bre_ref, bim_ref, cre_ref, cim_ref, p1re_ref, p1im_ref, pqre_ref, pqim_ref,
                  bare_ref, baim_ref, care_ref, caim_ref, ckre_ref, ckim_ref):
    ng = lr_ref.shape[0]
    gs = S5_GSIZE
    lr = lr_ref[...]
    li = li_ref[...]
    step = jnp.exp(ls_ref[...])
    mag = jnp.exp(lr * step)
    ab = (mag * jnp.cos(li * step), mag * jnp.sin(li * step))
    den = lr * lr + li * li
    nr = ab[0] - 1.0
    ni = ab[1]
    f = ((nr * lr + ni * li) / den, (ni * lr - nr * li) / den)
    p1re_ref[...] = ab[0]
    p1im_ref[...] = ab[1]
    pows = [(jnp.ones_like(lr), jnp.zeros_like(lr))]
    for _ in range(S5_Q):
        pows.append(_cmul(pows[-1], ab))
    r = pows[S5_Q]
    for k in range(SUBLANE):
        pqre_ref[k] = r[0]
        pqim_ref[k] = r[1]
        r = _cmul(r, pows[S5_Q])
    for i in range(S5_GSIZE):
        bb = _cmul(f, (bre_ref[i], bim_ref[i]))
        chan = pl.ds(i, ng, stride=gs)
        for s in range(S5_Q):
            v = _cmul(pows[S5_Q - 1 - s], bb)
            bare_ref[s, chan, :] = v[0]
            baim_ref[s, chan, :] = v[1]
        c = (cre_ref[i], cim_ref[i])
        for k in range(S5_Q + 1):
            v = _cmul(c, pows[k])
            care_ref[k, chan, :] = v[0]
            caim_ref[k, chan, :] = v[1]
            if k < S5_Q:
                lagged = pl.ds(k * gs + i, ng, stride=S5_Q * gs)
                ckre_ref[lagged, :] = v[0]
                ckim_ref[lagged, :] = v[1]


def _s5_disc(lam_re, lam_im, log_step, b_re_t, b_im_t, c_re_t, c_im_t):
    g, n = lam_re.shape
    gn = _sds((g, n))
    pq = _sds((SUBLANE, g, n))
    ba = _sds((S5_Q, g * S5_GSIZE, n))
    ca = _sds((S5_Q + 1, g * S5_GSIZE, n))
    ck = _sds((g * S5_Q * S5_GSIZE, n))
    return pl.pallas_call(
        _s5_disc_body,
        out_shape=[gn, gn, pq, pq, ba, ba, ca, ca, ck, ck],
        name="s5_disc",
    )(lam_re, lam_im, log_step, b_re_t, b_im_t, c_re_t, c_im_t)


def _hi_lo(a):
    hi = a.astype(BF16)
    return hi, (a - hi.astype(F32)).astype(BF16)


def _s5_kmat_body(bbr_ref, bbi_ref, ckr_ref, cki_ref, k_ref):
    dn = (((2,), (2,)), ((0,), (0,)))

    def mm(a, b):
        ah, al = _hi_lo(a)
        bh, bl = _hi_lo(b)
        d = lambda x, y: lax.dot_general(x, y, dn, preferred_element_type=F32)
        return d(ah, bh) + d(ah, bl) + d(al, bh)

    k_ref[...] = mm(bbr_ref[...], ckr_ref[...]) - mm(bbi_ref[...], cki_ref[...])


def _s5_kmat(bb_re, bb_im, ck_re, ck_im):
    g, m, _ = ck_re.shape
    return pl.pallas_call(
        _s5_kmat_body,
        out_shape=_sds((g, S5_GSIZE, m)),
        name="s5_kmat",
    )(bb_re, bb_im, ck_re, ck_im)


def _block_diag_lanes(a, width):
    rows, total = LANE, S5_LB * width
    rep = (lax.broadcasted_iota(jnp.int32, (width, total), 1) % width
           == lax.broadcasted_iota(jnp.int32, (width, total), 0)).astype(BF16)
    tiled = jnp.dot(a.astype(BF16), rep, preferred_element_type=F32)
    same_group = (lax.broadcasted_iota(jnp.int32, (rows, total), 0) // S5_GSIZE
                  == lax.broadcasted_iota(jnp.int32, (rows, total), 1) // width)
    return jnp.where(same_group, tiled, 0.0).astype(BF16)


def _s5_build_body(kt_ref, bar_ref, bai_ref, car_ref, cai_ref, tin_ref, wst_ref, wot_ref, c0t_ref):
    q = S5_Q
    gs = S5_GSIZE
    kt = kt_ref[...]
    lag_blocks = [_block_diag_lanes(kt[:, k * gs:(k + 1) * gs], gs) for k in range(q)]
    zero = jnp.zeros((LANE, LANE), BF16)
    for s in range(q):
        rs = slice(s * LANE, (s + 1) * LANE)
        for t in range(q):
            tin_ref[rs, t * LANE:(t + 1) * LANE] = lag_blocks[t - s] if t >= s else zero
        wst_ref[rs, 0:S5_LBN] = _block_diag_lanes(bar_ref[s], S5_STATE)
        wst_ref[rs, S5_LBN:] = _block_diag_lanes(bai_ref[s], S5_STATE)
        wot_ref[rs, 0:S5_LBN] = _block_diag_lanes(car_ref[s + 1], S5_STATE)
        wot_ref[rs, S5_LBN:] = _block_diag_lanes(-cai_ref[s + 1], S5_STATE)
    c0t_ref[:, 0:S5_LBN] = _block_diag_lanes(car_ref[0], S5_STATE)
    c0t_ref[:, S5_LBN:] = _block_diag_lanes(-cai_ref[0], S5_STATE)


def _s5_build(kt, ba_re, ba_im, ca_re, ca_im):
    nb, q = S5_GROUPS // S5_LB, S5_Q
    blk = lambda n: pl.BlockSpec((n, None, LANE, S5_STATE), lambda g: (0, g, 0, 0))
    big = pl.BlockSpec((None, q * LANE, 2 * S5_LBN), lambda g: (g, 0, 0))
    return pl.pallas_call(
        _s5_build_body,
        grid=(nb,),
        in_specs=[pl.BlockSpec((None, LANE, q * S5_GSIZE), lambda g: (g, 0, 0)), blk(q), blk(q), blk(q + 1), blk(q + 1)],
        out_specs=[pl.BlockSpec((None, q * LANE, q * LANE), lambda g: (g, 0, 0)), big, big,
                   pl.BlockSpec((None, LANE, 2 * S5_LBN), lambda g: (g, 0, 0))],
        out_shape=[_sds((nb, q * LANE, q * LANE), BF16), _sds((nb, q * LANE, 2 * S5_LBN), BF16),
                   _sds((nb, q * LANE, 2 * S5_LBN), BF16), _sds((nb, LANE, 2 * S5_LBN), BF16)],
        compiler_params=_cparams(1),
        name="s5_build",
    )(kt, ba_re, ba_im, ca_re, ca_im)


def _s5_operators(lam_re, lam_im, log_step, b_re, b_im, c_re, c_im):
    nb, q = S5_GROUPS // S5_LB, S5_Q
    g, gs, n = S5_GROUPS, S5_GSIZE, S5_STATE
    p1_re, p1_im, pq_re, pq_im, ba_re, ba_im, ca_re, ca_im, ck_re, ck_im = _s5_disc(
        lam_re, lam_im, log_step[:, None], b_re.transpose(2, 0, 1), b_im.transpose(2, 0, 1),
        c_re.transpose(1, 0, 2), c_im.transpose(1, 0, 2))
    kt = _s5_kmat(ba_re[q - 1].reshape(g, gs, n), ba_im[q - 1].reshape(g, gs, n),
                  ck_re.reshape(g, q * gs, n), ck_im.reshape(g, q * gs, n))
    by_block = lambda a: a.reshape(a.shape[0], nb, LANE, n)
    t_in, w_st, w_out_t, c_t = _s5_build(kt.reshape(nb, LANE, q * gs), by_block(ba_re), by_block(ba_im),
                                         by_block(ca_re), by_block(ca_im))

    lanes = lambda p: p.reshape(-1, nb, S5_LBN).transpose(1, 0, 2)
    rows = jnp.arange(SUBLANE)[None, :, None]
    pr, pi = lanes(pq_re), lanes(pq_im)
    tabs = [pr, pi]
    for d in (1, 2, 4):
        for p in (pr, pi):
            tabs.append(jnp.where(rows >= d, p[:, d - 1:d, :], 0.0))
    tab = jnp.stack(tabs, axis=1)
    abar = jnp.stack([lanes(p1_re), lanes(p1_im)], axis=1)
    return (t_in, w_st, w_out_t, tab), (w_st, c_t, abar)


def _s5_scan_tile(xr, xi, tab_ref, car_re, car_im):
    for d, k in ((1, 2), (2, 4), (4, 6)):
        ar = tab_ref[k]
        ai = tab_ref[k + 1]
        sr = pltpu.roll(xr, d, 0)
        si = pltpu.roll(xi, d, 0)
        xr, xi = xr + ar * sr - ai * si, xi + ar * si + ai * sr
    pr = tab_ref[0]
    pi = tab_ref[1]
    hr = xr + pr * car_re - pi * car_im
    hi = xi + pr * car_im + pi * car_re
    return hr, hi


def _s5_prompt_body(u_ref, tin_ref, wst_ref, wout_ref, tab_ref, d_ref, y_ref, hfin_ref, car, uf, yf):
    q = S5_Q
    j = pl.program_id(2)
    nj = pl.num_programs(2)
    nrow = u_ref.shape[0] // q

    @pl.when(j == 0)
    def _():
        car[...] = jnp.zeros(car.shape, F32)

    uf[...] = u_ref[...].astype(F32)
    xb = jnp.concatenate([uf[pl.ds(t, nrow, stride=q), :] for t in range(q)], axis=1).astype(BF16)
    s = jnp.dot(xb, wst_ref[...], preferred_element_type=F32)
    car_re = car[:, 0:S5_LBN]
    car_im = car[:, S5_LBN:]
    first = lax.broadcasted_iota(jnp.int32, (SUBLANE, S5_LBN), 0) == 0
    hp = []
    for t in range(nrow // SUBLANE):
        rs = slice(t * SUBLANE, (t + 1) * SUBLANE)
        hr, hi = _s5_scan_tile(s[rs, 0:S5_LBN], s[rs, S5_LBN:], tab_ref, car_re, car_im)
        hp.append(jnp.concatenate([jnp.where(first, car_re, pltpu.roll(hr, 1, 0)),
                                   jnp.where(first, car_im, pltpu.roll(hi, 1, 0))], axis=1))
        car_re = jnp.broadcast_to(hr[SUBLANE - 1:SUBLANE, :], hr.shape)
        car_im = jnp.broadcast_to(hi[SUBLANE - 1:SUBLANE, :], hi.shape)
    car[:, 0:S5_LBN] = car_re
    car[:, S5_LBN:] = car_im
    h_in = jnp.concatenate(hp, axis=0).astype(BF16)
    d_row = jnp.concatenate([d_ref[...]] * q, axis=1)
    y = (jnp.dot(xb, tin_ref[...], preferred_element_type=F32)
         + lax.dot_general(h_in, wout_ref[...], (((1,), (1,)), ((), ())), preferred_element_type=F32)
         + d_row * xb.astype(F32))
    y = _gelu_tanh(y)
    for t in range(q):
        yf[pl.ds(t, nrow, stride=q), :] = y[:, t * LANE:(t + 1) * LANE]
    y_ref[...] = yf[...].astype(y_ref.dtype)

    @pl.when(j == nj - 1)
    def _():
        hfin_ref[...] = car[...]


def _s5_prompt(proj, t_in, w_st, w_out, tab, d_skip, ntok):
    b, l, cols = proj.shape
    q = S5_Q
    nb = S5_GROUPS // S5_LB
    ub = P_U // LANE
    wmap = lambda g, i, j: (g, 0, 0)
    return pl.pallas_call(
        _s5_prompt_body,
        grid=(nb, b, l // ntok),
        in_specs=[pl.BlockSpec((None, ntok, LANE), lambda g, i, j: (i, j, ub + g)),
                  pl.BlockSpec((None, q * LANE, q * LANE), wmap),
                  pl.BlockSpec((None, q * LANE, 2 * S5_LBN), wmap),
                  pl.BlockSpec((None, 2 * S5_LBN, q * LANE), wmap),
                  pl.BlockSpec((None, 8, SUBLANE, S5_LBN), lambda g, i, j: (g, 0, 0, 0)),
                  pl.BlockSpec((1, LANE), lambda g, i, j: (0, g))],
        out_specs=[pl.BlockSpec((None, None, ntok, LANE), lambda g, i, j: (g, i, j, 0)),
                   pl.BlockSpec((None, None, SUBLANE, 2 * S5_LBN), lambda g, i, j: (i, g, 0, 0))],
        out_shape=[_sds((nb, b, l, LANE), BF16), _sds((b, nb, SUBLANE, 2 * S5_LBN))],
        scratch_shapes=[pltpu.VMEM((SUBLANE, 2 * S5_LBN), F32), pltpu.VMEM((ntok, LANE), F32),
                        pltpu.VMEM((ntok, LANE), F32)],
        compiler_params=_cparams(3),
        name="s5_prompt",
    )(proj, t_in, w_st, w_out, tab, d_skip)


def _rope_tables(pos):
    half = ROT_DIM // 2
    inv_freq = jnp.exp(-(2.0 * jnp.arange(half, dtype=F32) / ROT_DIM) * math.log(ROPE_THETA))
    ang = pos.astype(F32)[:, None] * inv_freq[None, :]
    cos, sin = jnp.cos(ang), jnp.sin(ang)
    l = pos.shape[0]
    one = jnp.ones((l, HEAD_DIM - ROT_DIM), F32)
    zero = jnp.zeros((l, HEAD_DIM - ROT_DIM), F32)
    zh = jnp.zeros((l, half), F32)
    cos_h = jnp.concatenate([cos, cos, one], axis=1)
    sa_h = jnp.concatenate([-sin, zh, zero], axis=1)
    sb_h = jnp.concatenate([zh, sin, zero], axis=1)
    two = lambda t: jnp.concatenate([t, t], axis=1)
    return two(cos_h), two(sa_h), two(sb_h)


def _rope_flat(x, cos, sa, sb):
    w = x.shape[1]
    n = w // LANE
    tile = lambda t: jnp.concatenate([t] * n, axis=1) if n > 1 else t
    half = ROT_DIM // 2
    return x * tile(cos) + pltpu.roll(x, w - half, 1) * tile(sa) + pltpu.roll(x, half, 1) * tile(sb)


def _attn_prompt_body(q_ref, k_ref, v_ref, cos_ref, sa_ref, sb_ref, sink_ref, wo_ref, y_ref, kn_ref, vn_ref,
                      kprev, vprev):
    j = pl.program_id(1)
    nb = pl.num_programs(1)
    w = WINDOW
    kvw = KV_HEADS * HEAD_DIM

    @pl.when(j == 0)
    def _():
        kprev[...] = jnp.zeros(kprev.shape, F32)
        vprev[...] = jnp.zeros(vprev.shape, F32)

    cos, sa, sb = cos_ref[...], sa_ref[...], sb_ref[...]
    q = _rope_flat(q_ref[...].astype(F32), cos, sa, sb) * (HEAD_DIM ** -0.5)
    k = _rope_flat(k_ref[...].astype(F32), cos, sa, sb)
    v = v_ref[...].astype(F32)
    kctx = jnp.concatenate([kprev[...], k], axis=0).astype(BF16)
    vctx = jnp.concatenate([vprev[...], v], axis=0).astype(BF16)
    rows = Q_PER_KV * w
    ti = lax.broadcasted_iota(jnp.int32, (rows, 2 * w), 0) % w
    ci = lax.broadcasted_iota(jnp.int32, (rows, 2 * w), 1)
    valid = (ci >= ti) & (ci <= ti + w) & ((ci >= w) | (j > 0))
    ones = jnp.ones((2 * w, LANE), BF16)
    qb = q.astype(BF16)
    outs = []
    for kv in range(KV_HEADS):
        kj = kctx[:, kv * HEAD_DIM:(kv + 1) * HEAD_DIM]
        vj = vctx[:, kv * HEAD_DIM:(kv + 1) * HEAD_DIM]
        heads = range(kv * Q_PER_KV, (kv + 1) * Q_PER_KV)
        qs = jnp.concatenate([qb[:, h * HEAD_DIM:(h + 1) * HEAD_DIM] for h in heads], axis=0)
        sink = jnp.concatenate([jnp.broadcast_to(sink_ref[0:1, h:h + 1], (w, 1)) for h in heads], axis=0)
        s = lax.dot_general(qs, kj, (((1,), (1,)), ((), ())), preferred_element_type=F32)
        s = jnp.where(valid, s, -jnp.inf)
        m = jnp.maximum(jnp.max(s, axis=-1, keepdims=True), sink)
        p = jnp.exp(s - m).astype(BF16)
        den = jnp.dot(p, ones, preferred_element_type=F32)[:, 0:HEAD_DIM] + jnp.exp(sink - m)
        o = jnp.dot(p, vj, preferred_element_type=F32) / den
        outs.extend(o[i * w:(i + 1) * w] for i in range(Q_PER_KV))
    o = jnp.concatenate(outs, axis=1)
    y_ref[...] = jnp.dot(o.astype(BF16), wo_ref[...], preferred_element_type=F32).astype(y_ref.dtype)
    kprev[...] = k
    vprev[...] = v

    @pl.when(j == nb - 1)
    def _():
        kn_ref[...] = k
        vn_ref[...] = v


def _attn_prompt(proj, cos, sa, sb, sinks, attn_o):
    b, l, _ = proj.shape
    d = D_MODEL
    w = WINDOW
    kvw = KV_HEADS * HEAD_DIM
    const2 = lambda i, j: (0, 0)
    tmap = lambda i, j: (j, 0)
    return pl.pallas_call(
        _attn_prompt_body,
        grid=(b, l // w),
        in_specs=[pl.BlockSpec((None, w, d), lambda i, j: (i, j, P_Q // d)),
                  pl.BlockSpec((None, w, kvw), lambda i, j: (i, j, P_K // kvw)),
                  pl.BlockSpec((None, w, kvw), lambda i, j: (i, j, P_V // kvw)),
                  pl.BlockSpec((w, LANE), tmap), pl.BlockSpec((w, LANE), tmap), pl.BlockSpec((w, LANE), tmap),
                  pl.BlockSpec((1, LANE), const2),
                  pl.BlockSpec((d, d), const2)],
        out_specs=[pl.BlockSpec((None, w, d), lambda i, j: (i, j, 0)),
                   pl.BlockSpec((None, w, kvw), lambda i, j: (i, 0, 0)),
                   pl.BlockSpec((None, w, kvw), lambda i, j: (i, 0, 0))],
        out_shape=[_sds((b, l, d), BF16), _sds((b, w, kvw)), _sds((b, w, kvw))],
        scratch_shapes=[pltpu.VMEM((w, kvw), F32), pltpu.VMEM((w, kvw), F32)],
        compiler_params=_cparams(2),
        name="attn_prompt",
    )(proj, proj, proj, cos, sa, sb, sinks, attn_o)


SAMPLE_TB = 8


def _mamba_sample_body(x_ref, bc_ref, dt_ref, cst_ref, ssm_ref, cw_ref, cb_ref, dtb_ref, alog_ref, dskip_ref,
                       exp_ref, stack_ref, y_ref, cnew_ref, ssmo_ref):
    del stack_ref
    tb = SAMPLE_TB
    new = jnp.concatenate([x_ref[...], bc_ref[...]], axis=1)
    conv = (cb_ref[...] + cw_ref[0:1, :] * cst_ref[0] + cw_ref[1:2, :] * cst_ref[1] + cw_ref[2:3, :] * cst_ref[2]
            + cw_ref[3:4, :] * new)
    cnew_ref[0] = cst_ref[1]
    cnew_ref[1] = cst_ref[2]
    cnew_ref[2] = new
    xbc = _silu(conv)
    xs = xbc[:, 0:M_INNER]
    gn = M_GROUPS * M_STATE
    bm = xbc[:, M_INNER:M_INNER + gn]
    cm = xbc[:, M_INNER + gn:].astype(BF16)
    dt = _softplus(dt_ref[...] + dtb_ref[...])
    a = dt * (-jnp.exp(alog_ref[...]))
    expand = exp_ref[...]
    dtx = _dot3_right(dt, expand) * xs
    da_x = jnp.exp(_dot3_right(a, expand))
    nl = M_INNER // LANE
    slab = jnp.concatenate([dtx[:, j * LANE:(j + 1) * LANE] for j in range(nl)]
                           + [da_x[:, j * LANE:(j + 1) * LANE] for j in range(nl)], axis=0)
    tt = slab.T
    hpg = M_HEADS // M_GROUPS
    gw = hpg * M_HEADDIM
    rowid = lax.broadcasted_iota(jnp.int32, (tb, gw), 0)
    ys = []
    for g in range(M_GROUPS):
        yacc = jnp.zeros((tb, gw), F32)
        for t in range(tb):
            parts = []
            for hl in range(hpg):
                h = g * hpg + hl
                j, hh = divmod(h * M_HEADDIM, LANE)
                col = j * tb + t
                dcol = tt[hh:hh + M_HEADDIM, col:col + 1]
                acol = tt[hh:hh + M_HEADDIM, nl * tb + col:nl * tb + col + 1]
                hn = acol * ssm_ref[t, h] + dcol * bm[t:t + 1, g * M_STATE:(g + 1) * M_STATE]
                ssmo_ref[t, h] = hn
                parts.append(hn)
            hng = jnp.concatenate(parts, axis=0).astype(BF16)
            yg = lax.dot_general(cm[:, g * M_STATE:(g + 1) * M_STATE], hng, (((1,), (1,)), ((), ())),
                                 preferred_element_type=F32)
            yacc = jnp.where(rowid == t, yg, yacc)
        ys.append(yacc)
    y_ref[...] = jnp.concatenate(ys, axis=1) + xs * dskip_ref[...]


def _mamba_sample(proj, dtp, conv_t, ssm, conv_w, conv_b, dt_bias, a_log, d_skip, expand, ssm_stack, li):
    s = proj.shape[0]
    d = D_MODEL
    tb = SAMPLE_TB
    const = lambda i: (0, 0)
    return pl.pallas_call(
        _mamba_sample_body,
        grid=(s // tb,),
        in_specs=[pl.BlockSpec((tb, d), lambda i: (i, S_X // d)),
                  pl.BlockSpec((tb, d), lambda i: (i, S_BC // d)),
                  pl.BlockSpec((tb, LANE), lambda i: (i, 0)),
                  pl.BlockSpec((CONV_W - 1, tb, CONV_DIM), lambda i: (0, i, 0)),
                  pl.BlockSpec((None, tb, M_HEADS, M_HEADDIM, M_STATE), lambda i: (li, i, 0, 0, 0)),
                  pl.BlockSpec((CONV_W, CONV_DIM), const), pl.BlockSpec((1, CONV_DIM), const),
                  pl.BlockSpec((1, LANE), const), pl.BlockSpec((1, LANE), const), pl.BlockSpec((1, d), const),
                  pl.BlockSpec((LANE, d), const),
                  pl.BlockSpec(memory_space=pl.ANY)],
        out_specs=[pl.BlockSpec((tb, d), lambda i: (i, 0)),
                   pl.BlockSpec((CONV_W - 1, tb, CONV_DIM), lambda i: (0, i, 0)),
                   pl.BlockSpec((None, tb, M_HEADS, M_HEADDIM, M_STATE), lambda i: (li, i, 0, 0, 0))],
        out_shape=[_sds((s, d)), _sds((CONV_W - 1, s, CONV_DIM)), _sds(ssm_stack.shape)],
        input_output_aliases={11: 2},
        compiler_params=_cparams(1),
        name="mamba_sample",
    )(proj, proj, dtp, conv_t, ssm, conv_w, conv_b, dt_bias, a_log, d_skip, expand, ssm_stack)


def _gate_norm_proj_body(y_ref, z_ref, nw_ref, w_ref, o_ref):
    y = _group_norm_gate(y_ref[...], z_ref[...], nw_ref[...])
    o_ref[...] = jnp.dot(y.astype(BF16), w_ref[...], preferred_element_type=F32)


def _gate_norm_proj(y, proj, m_norm_w, m_proj):
    s, d = y.shape
    const = lambda i: (0, 0)
    return pl.pallas_call(
        _gate_norm_proj_body,
        grid=(1,),
        in_specs=[pl.BlockSpec((s, d), const), pl.BlockSpec((s, d), lambda i: (0, S_Z // d)),
                  pl.BlockSpec((1, d), const), pl.BlockSpec((d, d), const)],
        out_specs=pl.BlockSpec((s, d), const),
        out_shape=_sds((s, d)),
        compiler_params=_cparams(1),
        name="gate_norm_proj",
    )(y, proj, m_norm_w, m_proj)


def _s5_sample_body(u_ref, hre_ref, him_ref, wb_ref, ct_ref, tab_ref, d_ref, y_ref, ore_ref, oim_ref):
    u = u_ref[...]
    bu = jnp.dot(u.astype(BF16), wb_ref[...], preferred_element_type=F32)
    bre = bu[:, 0:S5_LBN]
    bim = bu[:, S5_LBN:]
    ar = tab_ref[0]
    ai = tab_ref[1]
    h0r = hre_ref[...]
    h0i = him_ref[...]
    hr = ar * h0r - ai * h0i + bre
    hi = ar * h0i + ai * h0r + bim
    ore_ref[...] = hr
    oim_ref[...] = hi
    h = jnp.concatenate([hr, hi], axis=1).astype(BF16)
    y = lax.dot_general(h, ct_ref[...], (((1,), (1,)), ((), ())), preferred_element_type=F32) + d_ref[...] * u
    y_ref[...] = _gelu_tanh(y)


def _s5_sample(proj, h_re, h_im, w_st, c_t, tab, d_skip):
    s = proj.shape[0]
    nb = S5_GROUPS // S5_LB
    ub = S_U // LANE
    wmap = lambda g: (g, 0, 0)
    hspec = pl.BlockSpec((s, S5_LBN), lambda g: (0, g))
    return pl.pallas_call(
        _s5_sample_body,
        grid=(nb,),
        in_specs=[pl.BlockSpec((s, LANE), lambda g: (0, ub + g)), hspec, hspec,
                  pl.BlockSpec((None, LANE, 2 * S5_LBN), lambda g: (g, S5_Q - 1, 0)),
                  pl.BlockSpec((None, LANE, 2 * S5_LBN), wmap),
                  pl.BlockSpec((None, 2, 1, S5_LBN), lambda g: (g, 0, 0, 0)),
                  pl.BlockSpec((1, LANE), lambda g: (0, g))],
        out_specs=[pl.BlockSpec((None, s, LANE), wmap), hspec, hspec],
        out_shape=[_sds((nb, s, LANE)), _sds(h_re.shape), _sds(h_im.shape)],
        compiler_params=_cparams(1),
        name="s5_sample",
    )(proj, h_re, h_im, w_st, c_t, tab, d_skip)


def _rope_sample_body(q_ref, k_ref, cos_ref, sa_ref, sb_ref, qo_ref, ko_ref):
    cos, sa, sb = cos_ref[...], sa_ref[...], sb_ref[...]
    qo_ref[...] = _rope_flat(q_ref[...], cos, sa, sb) * (HEAD_DIM ** -0.5)
    ko_ref[...] = _rope_flat(k_ref[...], cos, sa, sb)


def _rope_sample(proj, cos, sa, sb):
    s = proj.shape[0]
    kvw = KV_HEADS * HEAD_DIM
    const = lambda i: (0, 0)
    return pl.pallas_call(
        _rope_sample_body,
        grid=(1,),
        in_specs=[pl.BlockSpec((s, QX), lambda i: (0, S_QX // QX)), pl.BlockSpec((s, kvw), lambda i: (0, S_K // kvw)),
                  pl.BlockSpec((1, LANE), const), pl.BlockSpec((1, LANE), const), pl.BlockSpec((1, LANE), const)],
        out_specs=[pl.BlockSpec((s, QX), const), pl.BlockSpec((s, kvw), const)],
        out_shape=[_sds((s, QX)), _sds((s, kvw))],
        compiler_params=_cparams(1),
        name="rope_sample",
    )(proj, proj, cos, sa, sb)


def _attn_sample_body(q_ref, kc_ref, vc_ref, kn_ref, vn_ref, sink_ref, kstack_ref, vstack_ref, o_ref, ko_ref, vo_ref):
    del kstack_ref, vstack_ref
    q = q_ref[...]
    kc = kc_ref[...]
    vc = vc_ref[...]
    kn = kn_ref[...]
    vn = vn_ref[...]
    w = kc.shape[1]
    s_c = lax.dot_general(q.astype(BF16), kc.astype(BF16), (((2,), (2,)), ((0,), (0,))),
                          preferred_element_type=F32)
    s_n = jnp.sum(q * kn, axis=-1, keepdims=True)
    sink = sink_ref[...][None]
    m = jnp.maximum(jnp.maximum(jnp.max(s_c, axis=-1, keepdims=True), s_n), sink)
    p_c = jnp.exp(s_c - m)
    p_n = jnp.exp(s_n - m)
    den = jnp.sum(p_c, axis=-1, keepdims=True) + p_n + jnp.exp(sink - m)
    o = lax.dot_general((p_c / den).astype(BF16), vc.astype(BF16), (((2,), (1,)), ((0,), (0,))),
                        preferred_element_type=F32)
    o_ref[...] = o + (p_n / den) * vn
    ko_ref[:, 0:w - 1, :] = kc[:, 1:w, :]
    ko_ref[:, w - 1:w, :] = kn
    vo_ref[:, 0:w - 1, :] = vc[:, 1:w, :]
    vo_ref[:, w - 1:w, :] = vn


def _attn_sample(qx, kc, vc, kn, proj3, sinks, k_stack, v_stack, li):
    s, w, kvw = kc.shape
    tb = SAMPLE_TB
    row3 = lambda i: (i, 0, 0)
    slab = pl.BlockSpec((None, tb, w, kvw), lambda i: (li, i, 0, 0))
    return pl.pallas_call(
        _attn_sample_body,
        grid=(s // tb,),
        in_specs=[pl.BlockSpec((tb, A_HEADS, kvw), row3), pl.BlockSpec((tb, w, kvw), row3),
                  pl.BlockSpec((tb, w, kvw), row3), pl.BlockSpec((tb, 1, kvw), row3),
                  pl.BlockSpec((tb, 1, kvw), lambda i: (i, 0, S_V // kvw)),
                  pl.BlockSpec((A_HEADS, 1), lambda i: (0, 0)),
                  pl.BlockSpec(memory_space=pl.ANY), pl.BlockSpec(memory_space=pl.ANY)],
        out_specs=[pl.BlockSpec((tb, A_HEADS, kvw), row3), slab, slab],
        out_shape=[_sds((s, A_HEADS, kvw)), _sds(k_stack.shape), _sds(v_stack.shape)],
        input_output_aliases={6: 1, 7: 2},
        compiler_params=_cparams(1),
        name="attn_sample",
    )(qx, kc, vc, kn, proj3, sinks, k_stack, v_stack)


def _dense_body(x_ref, w_ref, o_ref):
    o_ref[...] = jnp.dot(x_ref[...].astype(BF16), w_ref[...], preferred_element_type=F32)


def _dense(x, w, tn):
    m, k = x.shape
    n = w.shape[1]
    return pl.pallas_call(
        _dense_body,
        grid=(n // tn,),
        in_specs=[pl.BlockSpec((m, k), lambda j: (0, 0)), pl.BlockSpec((k, tn), lambda j: (0, j))],
        out_specs=pl.BlockSpec((m, tn), lambda j: (0, j)),
        out_shape=_sds((m, n)),
        compiler_params=_cparams(1),
        name="dense",
    )(x, w)


def _pad_lanes(v):
    return jnp.zeros((1, LANE), F32).at[0, :v.shape[0]].set(v.astype(F32))


def _projection_weights(w_in, attn_o):
    depth = w_in.shape[0]
    w = w_in.astype(BF16)
    seg = lambda a, b: w[:, :, a:b]
    z, xbc, dt, u = seg(OFF_Z, OFF_XBC), seg(OFF_XBC, OFF_DT), seg(OFF_DT, OFF_U), seg(OFF_U, OFF_Q)
    q, k, v, g = seg(OFF_Q, OFF_K), seg(OFF_K, OFF_V), seg(OFF_V, OFF_G), seg(OFF_G, IN_COLS)
    w_prompt = jnp.concatenate([z, xbc, u, q, g, k, v], axis=2)
    sel = (jnp.arange(A_HEADS)[:, None] // Q_PER_KV == jnp.arange(KV_HEADS)[None, :]).astype(BF16)
    qx = (q.reshape(depth, D_MODEL, A_HEADS, 1, HEAD_DIM) * sel[None, None, :, :, None]).reshape(depth, D_MODEL, QX)
    w_sample = jnp.concatenate([z, xbc, u, qx, g, k, v], axis=2)
    w_dt = jnp.pad(dt, ((0, 0), (0, 0), (0, LANE - M_HEADS)))
    ao = attn_o.astype(BF16).reshape(depth, A_HEADS, 1, HEAD_DIM, D_MODEL) * sel[None, :, :, None, None]
    return w_prompt, w_sample, w_dt, ao.reshape(depth, QX, D_MODEL)


def kernel(x_prompt, x_sample, state_ssm, state_conv, state_s5_re, state_s5_im, cache_k, cache_v, norm1_w, w_in,
           conv_w, conv_b, dt_bias, a_log, m_d, m_norm_w, m_proj, s5_lam_re, s5_lam_im, s5_log_step, s5_b_re,
           s5_b_im, s5_c_re, s5_c_im, s5_d, s5_glu_w, attn_sinks, attn_o, w_out, norm2_w, mlp_up, mlp_down,
           final_norm_w):
    b, l, d = x_prompt.shape
    s = x_sample.shape[0]
    kvw = KV_HEADS * HEAD_DIM
    nb = S5_GROUPS // S5_LB
    xp = x_prompt.reshape(b * l, d)
    xs = x_sample.reshape(s, d)
    cos_p, sa_p, sb_p = _rope_tables(jnp.arange(l, dtype=jnp.int32))
    cos_s, sa_s, sb_s = _rope_tables(jnp.full((1,), PAST_LEN, jnp.int32))
    expand = (jnp.arange(LANE)[:, None] == jnp.arange(M_INNER)[None, :] // M_HEADDIM).astype(BF16)
    fnw = final_norm_w[None]
    w_prompt_all, w_sample_all, w_dt_all, ao_x_all = _projection_weights(w_in, attn_o)
    ssm_s = jnp.zeros((DEPTH,) + state_ssm.shape[1:], F32)
    k_s = jnp.zeros((DEPTH, s, WINDOW, kvw), F32)
    v_s = jnp.zeros((DEPTH, s, WINDOW, kvw), F32)
    outs = [[] for _ in range(9)]
    for li in range(DEPTH):
        final = li == DEPTH - 1
        w_prompt, w_sample, w_dt, ao_x = w_prompt_all[li], w_sample_all[li], w_dt_all[li], ao_x_all[li]
        n1 = norm1_w[li][None]
        cw, cb = conv_w[li], conv_b[li][None]
        dtb, alog = _pad_lanes(dt_bias[li]), _pad_lanes(a_log[li])
        dsk = jnp.repeat(m_d[li], M_HEADDIM)[None]
        mnw = m_norm_w[li][None]
        mpj = m_proj[li].astype(BF16)
        s5_chunked, s5_step = _s5_operators(s5_lam_re[li], s5_lam_im[li], s5_log_step[li], s5_b_re[li], s5_b_im[li],
                                            s5_c_re[li], s5_c_im[li])
        s5d = s5_d[li][None]
        sinks = attn_sinks[li]
        ao = attn_o[li].astype(BF16)
        mlp_w = (s5_glu_w[li].astype(BF16), w_out[li].astype(BF16), norm2_w[li][None], mlp_up[li].astype(BF16),
                 mlp_down[li].astype(BF16), fnw)

        proj, dtp = _norm_proj(xp, n1, w_prompt, w_dt, 2048, 512, BF16)
        proj3 = proj.reshape(b, l, P_COLS)
        ym, conv_p, ssm_p = _mamba_prompt(proj3, dtp.reshape(b, l, LANE), cw, cb, dtb, alog, dsk, mnw, expand, mpj)
        ys, h_p = _s5_prompt(proj3, *s5_chunked, s5d, 4096)
        hre_p, him_p = h_p[:, :, 0, :S5_LBN], h_p[:, :, 0, S5_LBN:]
        ya, k_p, v_p = _attn_prompt(proj3, cos_p, sa_p, sb_p, _pad_lanes(sinks), ao)
        xp = _merge_mlp(xp, proj, P_G, ym.reshape(b * l, d), ys.reshape(nb, b * l, LANE), ya.reshape(b * l, d),
                        *mlp_w, 512, final)

        sproj, sdt = _norm_proj(xs, n1, w_sample, w_dt, s, 512, F32)
        y_pre, conv_t, ssm_s = _mamba_sample(sproj, sdt, state_conv[li].transpose(1, 0, 2), state_ssm, cw, cb,
                                             dtb, alog, dsk, expand, ssm_s, li)
        sym = _gate_norm_proj(y_pre, sproj, mnw, mpj)
        sys_, hre_s, him_s = _s5_sample(sproj, state_s5_re[li].reshape(s, S5_GROUPS * S5_STATE),
                                        state_s5_im[li].reshape(s, S5_GROUPS * S5_STATE), *s5_step, s5d)
        qx_rot, k_rot = _rope_sample(sproj, cos_s, sa_s, sb_s)
        o, k_s, v_s = _attn_sample(qx_rot.reshape(s, A_HEADS, kvw), cache_k[li].reshape(s, WINDOW, kvw),
                                   cache_v[li].reshape(s, WINDOW, kvw), k_rot.reshape(s, 1, kvw),
                                   sproj.reshape(s, 1, S_COLS), sinks[:, None], k_s, v_s, li)
        sya = _dense(o.reshape(s, QX), ao_x, 512)
        xs = _merge_mlp(xs, sproj, S_G, sym, sys_, sya, *mlp_w, s, final)

        for i, val in enumerate((
                ssm_p, conv_p, conv_t.transpose(1, 0, 2),
                hre_p.reshape(b, S5_GROUPS, S5_STATE), hre_s.reshape(s, S5_GROUPS, S5_STATE),
                him_p.reshape(b, S5_GROUPS, S5_STATE), him_s.reshape(s, S5_GROUPS, S5_STATE),
                k_p.reshape(b, WINDOW, KV_HEADS, HEAD_DIM), v_p.reshape(b, WINDOW, KV_HEADS, HEAD_DIM))):
            outs[i].append(val)
    ssm_p, conv_p, conv_s, hre_p, hre_s, him_p, him_s, k_p, v_p = (jnp.stack(o) for o in outs)
    cache_shape = (DEPTH, s, WINDOW, KV_HEADS, HEAD_DIM)
    return (xp.reshape(b, l, d), xs.reshape(s, 1, d), ssm_p, ssm_s, conv_p, conv_s, hre_p, hre_s, him_p, him_s,
            k_p, k_s.reshape(cache_shape), v_p, v_s.reshape(cache_shape))
```

```python
import functools
import math

import jax
import jax.numpy as jnp
from jax import lax
from jax.experimental import pallas as pl
from jax.experimental.pallas import tpu as pltpu

F32 = jnp.float32
BF16 = jnp.bfloat16

D_MODEL = 1024
DEPTH = 4
PAST_LEN = 8192
M_HEADDIM = 64
M_HEADS = 16
M_INNER = 1024
M_GROUPS = 4
M_STATE = 128
CONV_W = 4
CONV_DIM = 2048
CHUNK = 128
S5_GROUPS = 64
S5_GSIZE = 16
S5_STATE = 64
HEAD_DIM = 64
A_HEADS = 16
KV_HEADS = 4
Q_PER_KV = 4
ROT_DIM = 16
ROPE_THETA = 500000.0
WINDOW = 128
D_FF = 4096
EPS = 1e-6

OFF_Z, OFF_XBC, OFF_DT, OFF_U, OFF_Q, OFF_K, OFF_V, OFF_G, IN_COLS = 0, 1024, 3072, 3088, 4112, 5136, 5392, 5648, 8720

LANE = 128
SUBLANE = 8
S5_LB = 8
S5_LBN = S5_LB * S5_STATE
VMEM_LIMIT = 56 * 1024 * 1024

P_Z, P_X, P_BC, P_U, P_Q, P_G, P_K, P_V, P_COLS = 0, 1024, 2048, 3072, 4096, 5120, 8192, 8448, 8704
S_Z, S_X, S_BC, S_U, S_QX, S_G, S_K, S_V, S_COLS = 0, 1024, 2048, 3072, 4096, 8192, 11264, 11520, 11776
QX = A_HEADS * KV_HEADS * HEAD_DIM


def _sds(shape, dtype=F32):
    return jax.ShapeDtypeStruct(shape, dtype)


def _cparams(n_axes):
    return pltpu.CompilerParams(dimension_semantics=("arbitrary",) * n_axes, vmem_limit_bytes=VMEM_LIMIT)


def _sigmoid(x):
    return 0.5 * (1.0 + jnp.tanh(0.5 * x))


def _silu(x):
    return x * _sigmoid(x)


def _softplus(x):
    return jnp.maximum(x, 0.0) + jnp.log(1.0 + jnp.exp(-jnp.abs(x)))


def _gelu_tanh(x):
    return 0.5 * x * (1.0 + jnp.tanh(math.sqrt(2.0 / math.pi) * (x + 0.044715 * (x * x * x))))


def _bdot(a, b):
    return jnp.dot(a.astype(BF16), b.astype(BF16), preferred_element_type=F32)


def _split3(a):
    hi = a.astype(BF16)
    r = a - hi.astype(F32)
    mid = r.astype(BF16)
    lo = (r - mid.astype(F32)).astype(BF16)
    return hi, mid, lo


def _dot3_right(a, sel):
    hi, mid, lo = _split3(a)
    return (jnp.dot(hi, sel, preferred_element_type=F32) + jnp.dot(mid, sel, preferred_element_type=F32)
            + jnp.dot(lo, sel, preferred_element_type=F32))


def _dot3_left(sel, a):
    hi, mid, lo = _split3(a)
    return (jnp.dot(sel, hi, preferred_element_type=F32) + jnp.dot(sel, mid, preferred_element_type=F32)
            + jnp.dot(sel, lo, preferred_element_type=F32))


def _rms(x, w):
    return x * lax.rsqrt(jnp.mean(x * x, axis=-1, keepdims=True) + EPS) * w


def _norm_proj_body(x_ref, nw_ref, w_ref, wdt_ref, o_ref, odt_ref, h_scr):
    @pl.when(pl.program_id(1) == 0)
    def _():
        hb = _rms(x_ref[...], nw_ref[...]).astype(BF16)
        h_scr[...] = hb
        odt_ref[...] = jnp.dot(hb, wdt_ref[...], preferred_element_type=F32)

    o_ref[...] = jnp.dot(h_scr[...], w_ref[...], preferred_element_type=F32).astype(o_ref.dtype)


def _norm_proj(x, nw, w, wdt, tm, tn, out_dtype):
    m, d = x.shape
    n = w.shape[1]
    return pl.pallas_call(
        _norm_proj_body,
        grid=(m // tm, n // tn),
        in_specs=[pl.BlockSpec((tm, d), lambda i, j: (i, 0)),
                  pl.BlockSpec((1, d), lambda i, j: (0, 0)),
                  pl.BlockSpec((d, tn), lambda i, j: (0, j)),
                  pl.BlockSpec((d, LANE), lambda i, j: (0, 0))],
        out_specs=[pl.BlockSpec((tm, tn), lambda i, j: (i, j)),
                   pl.BlockSpec((tm, LANE), lambda i, j: (i, 0))],
        out_shape=[_sds((m, n), out_dtype), _sds((m, LANE))],
        scratch_shapes=[pltpu.VMEM((tm, d), BF16)],
        compiler_params=_cparams(2),
        name="norm_proj",
    )(x, nw, w, wdt)


def _merge_mlp_body(final, x_ref, g0_ref, g1_ref, g2_ref, ym_ref, ys_ref, ya_ref, glu_ref, wo_ref, n2_ref, up_ref,
                    dn_ref, fn_ref, o_ref):
    ys = jnp.concatenate([ys_ref[g] for g in range(ys_ref.shape[0])], axis=1)
    glu = jnp.dot(ys.astype(BF16), glu_ref[...], preferred_element_type=F32)
    y_s = glu[:, :D_MODEL] * _sigmoid(glu[:, D_MODEL:])
    f32 = lambda r: r[...].astype(F32)
    merged = _sigmoid(f32(g0_ref)) * f32(ym_ref) + _sigmoid(f32(g1_ref)) * y_s + _sigmoid(f32(g2_ref)) * f32(ya_ref)
    x1 = x_ref[...] + jnp.dot(merged.astype(BF16), wo_ref[...], preferred_element_type=F32)
    h2 = _rms(x1, n2_ref[...]).astype(BF16)
    acc = x1
    fc = 1024
    for c in range(D_FF // fc):
        a = jnp.dot(h2, up_ref[:, c * fc:(c + 1) * fc], preferred_element_type=F32)
        a = jnp.square(jnp.maximum(a, 0.0))
        acc = acc + jnp.dot(a.astype(BF16), dn_ref[c * fc:(c + 1) * fc, :], preferred_element_type=F32)
    o_ref[...] = _rms(acc, fn_ref[...]) if final else acc


def _merge_mlp(x, proj, g_col, ym, ys, ya, glu_w, w_out, n2, up, dn, fnw, tm, final):
    m, d = x.shape
    gb = g_col // d
    nlb = d // LANE
    row = lambda i: (i, 0)
    const = lambda i: (0, 0)
    wspec = lambda shape: pl.BlockSpec(shape, const, pipeline_mode=pl.Buffered(1))
    return pl.pallas_call(
        functools.partial(_merge_mlp_body, final),
        grid=(m // tm,),
        in_specs=[pl.BlockSpec((tm, d), row),
                  pl.BlockSpec((tm, d), lambda i: (i, gb)),
                  pl.BlockSpec((tm, d), lambda i: (i, gb + 1)),
                  pl.BlockSpec((tm, d), lambda i: (i, gb + 2)),
                  pl.BlockSpec((tm, d), row), pl.BlockSpec((nlb, tm, LANE), lambda i: (0, i, 0)),
                  pl.BlockSpec((tm, d), row),
                  wspec((d, 2 * d)), wspec((d, d)), wspec((1, d)), wspec((d, D_FF)), wspec((D_FF, d)),
                  wspec((1, d))],
        out_specs=pl.BlockSpec((tm, d), row),
        out_shape=_sds((m, d)),
        compiler_params=_cparams(1),
        name="merge_mlp",
    )(x, proj, proj, proj, ym, ys, ya, glu_w, w_out, n2, up, dn, fnw)


def _group_norm_gate(y, z, nw):
    y = y * _silu(z)
    gw = M_INNER // M_GROUPS
    parts = []
    for g in range(M_GROUPS):
        yg = y[:, g * gw:(g + 1) * gw]
        parts.append(yg * lax.rsqrt(jnp.mean(yg * yg, axis=-1, keepdims=True) + EPS))
    return jnp.concatenate(parts, axis=1) * nw


def _mamba_prompt_body(z_ref, x_ref, bc_ref, dt_ref, cw_ref, cb_ref, dtb_ref, alog_ref, dskip_ref, nw_ref,
                       exp_ref, mproj_ref, y_ref, conv_ref, ssm_ref, xprev, ht):
    c = pl.program_id(1)
    nc = pl.num_programs(1)
    q = CHUNK
    tail = xprev.shape[0]

    @pl.when(c == 0)
    def _():
        xprev[...] = jnp.zeros(xprev.shape, xprev.dtype)
        ht[...] = jnp.zeros(ht.shape, F32)

    xcur = jnp.concatenate([x_ref[...], bc_ref[...]], axis=1)
    xext = jnp.concatenate([xprev[...], xcur], axis=0)
    ri = lax.broadcasted_iota(jnp.int32, (q, tail + q), 0)
    ci = lax.broadcasted_iota(jnp.int32, (q, tail + q), 1)
    conv = cb_ref[...] + cw_ref[CONV_W - 1:CONV_W, :] * xcur.astype(F32)
    for k in range(CONV_W - 1):
        shift = (ci == ri + (tail - (CONV_W - 1) + k)).astype(BF16)
        conv = conv + cw_ref[k:k + 1, :] * jnp.dot(shift, xext, preferred_element_type=F32)
    last_rows = xcur[q - tail:q, :]
    xprev[...] = last_rows

    @pl.when(c == nc - 1)
    def _():
        conv_ref[...] = last_rows.astype(F32)[tail - (CONV_W - 1):tail, :]

    xbc = _silu(conv)
    xs = xbc[:, 0:M_INNER]
    gn = M_GROUPS * M_STATE
    bm = xbc[:, M_INNER:M_INNER + gn]
    cm = xbc[:, M_INNER + gn:]

    dt = _softplus(dt_ref[...] + dtb_ref[...])
    a = dt * (-jnp.exp(alog_ref[...]))
    ri = lax.broadcasted_iota(jnp.int32, (q, q), 0)
    ci = lax.broadcasted_iota(jnp.int32, (q, q), 1)
    causal = ci <= ri
    tri = causal.astype(BF16)
    acum = _dot3_left(tri, a)
    expand = exp_ref[...]
    acum_x = _dot3_right(acum, expand)
    dt_x = _dot3_right(dt, expand)
    acum_last = acum_x[q - 1:q, :]
    xdt = xs * dt_x
    xw = xs * (jnp.exp(acum_last - acum_x) * dt_x)
    e_acum = jnp.exp(acum_x)
    chunk_decay = jnp.exp(acum_last)
    acum_t = acum.T

    hpg = M_HEADS // M_GROUPS
    gw = hpg * M_HEADDIM
    y_parts = []
    for g in range(M_GROUPS):
        cg = cm[:, g * M_STATE:(g + 1) * M_STATE].astype(BF16)
        bg = bm[:, g * M_STATE:(g + 1) * M_STATE]
        bgb = bg.astype(BF16)
        cb = lax.dot_general(cg, bgb, (((1,), (1,)), ((), ())), preferred_element_type=F32)
        h_prev = ht[g]
        y_off = jnp.dot(cg, h_prev.astype(BF16), preferred_element_type=F32) * e_acum[:, g * gw:(g + 1) * gw]
        yd = []
        for hl in range(hpg):
            h = g * hpg + hl
            seg = acum[:, h:h + 1] - acum_t[h:h + 1, :]
            lmat = jnp.exp(jnp.where(causal, seg, -jnp.inf))
            mh = (cb * lmat).astype(BF16)
            yd.append(jnp.dot(mh, xdt[:, h * M_HEADDIM:(h + 1) * M_HEADDIM].astype(BF16),
                              preferred_element_type=F32))
        y_parts.append(jnp.concatenate(yd, axis=1) + y_off)
        s_new = jnp.dot(bg.T.astype(BF16), xw[:, g * gw:(g + 1) * gw].astype(BF16), preferred_element_type=F32)
        ht[g] = h_prev * chunk_decay[:, g * gw:(g + 1) * gw] + s_new
    y = jnp.concatenate(y_parts, axis=1) + xs * dskip_ref[...]
    y = _group_norm_gate(y, z_ref[...].astype(F32), nw_ref[...])
    y_ref[...] = jnp.dot(y.astype(BF16), mproj_ref[...], preferred_element_type=F32).astype(y_ref.dtype)

    @pl.when(c == nc - 1)
    def _():
        for g in range(M_GROUPS):
            ssm_ref[g * hpg:(g + 1) * hpg] = ht[g].T.reshape(hpg, M_HEADDIM, M_STATE)


def _mamba_prompt(proj, dtp, conv_w, conv_b, dt_bias, a_log, d_skip, m_norm_w, expand, m_proj):
    b, l, _ = proj.shape
    d = D_MODEL
    nc = l // CHUNK
    const2 = lambda i, j: (0, 0)
    return pl.pallas_call(
        _mamba_prompt_body,
        grid=(b, nc),
        in_specs=[pl.BlockSpec((None, CHUNK, d), lambda i, j: (i, j, P_Z // d)),
                  pl.BlockSpec((None, CHUNK, d), lambda i, j: (i, j, P_X // d)),
                  pl.BlockSpec((None, CHUNK, d), lambda i, j: (i, j, P_BC // d)),
                  pl.BlockSpec((None, CHUNK, LANE), lambda i, j: (i, j, 0)),
                  pl.BlockSpec((CONV_W, CONV_DIM), const2),
                  pl.BlockSpec((1, CONV_DIM), const2),
                  pl.BlockSpec((1, LANE), const2),
                  pl.BlockSpec((1, LANE), const2),
                  pl.BlockSpec((1, d), const2),
                  pl.BlockSpec((1, d), const2),
                  pl.BlockSpec((LANE, d), const2),
                  pl.BlockSpec((d, d), const2)],
        out_specs=[pl.BlockSpec((None, CHUNK, d), lambda i, j: (i, j, 0)),
                   pl.BlockSpec((None, CONV_W - 1, CONV_DIM), lambda i, j: (i, 0, 0)),
                   pl.BlockSpec((None, M_HEADS, M_HEADDIM, M_STATE), lambda i, j: (i, 0, 0, 0))],
        out_shape=[_sds((b, l, d), BF16), _sds((b, CONV_W - 1, CONV_DIM)), _sds((b, M_HEADS, M_HEADDIM, M_STATE))],
        scratch_shapes=[pltpu.VMEM((2 * SUBLANE, CONV_DIM), BF16),
                        pltpu.VMEM((M_GROUPS, M_STATE, (M_HEADS // M_GROUPS) * M_HEADDIM), F32)],
        compiler_params=_cparams(2),
        name="mamba_prompt",
    )(proj, proj, proj, dtp, conv_w, conv_b, dt_bias, a_log, d_skip, m_norm_w, expand, m_proj)


S5_Q = 8


def _cmul(a, b):
    return a[0] * b[0] - a[1] * b[1], a[0] * b[1] + a[1] * b[0]


def _s5_disc_body(lr_ref, li_ref, ls_ref, bre_ref, bim_ref, cre_ref, cim_ref, p1re_ref, p1im_ref, pqre_ref, pqim_ref,
                  bare_ref, baim_ref, care_ref, caim_ref, ckre_ref, ckim_ref):
    ng = lr_ref.shape[0]
    gs = S5_GSIZE
    lr = lr_ref[...]
    li = li_ref[...]
    step = jnp.exp(ls_ref[...])
    mag = jnp.exp(lr * step)
    ab = (mag * jnp.cos(li * step), mag * jnp.sin(li * step))
    den = lr * lr + li * li
    nr = ab[0] - 1.0
    ni = ab[1]
    f = ((nr * lr + ni * li) / den, (ni * lr - nr * li) / den)
    p1re_ref[...] = ab[0]
    p1im_ref[...] = ab[1]
    pows = [(jnp.ones_like(lr), jnp.zeros_like(lr))]
    for _ in range(S5_Q):
        pows.append(_cmul(pows[-1], ab))
    r = pows[S5_Q]
    for k in range(SUBLANE):
        pqre_ref[k] = r[0]
        pqim_ref[k] = r[1]
        r = _cmul(r, pows[S5_Q])
    for i in range(S5_GSIZE):
        bb = _cmul(f, (bre_ref[i], bim_ref[i]))
        chan = pl.ds(i, ng, stride=gs)
        for s in range(S5_Q):
            v = _cmul(pows[S5_Q - 1 - s], bb)
            bare_ref[s, chan, :] = v[0]
            baim_ref[s, chan, :] = v[1]
        c = (cre_ref[i], cim_ref[i])
        for k in range(S5_Q + 1):
            v = _cmul(c, pows[k])
            care_ref[k, chan, :] = v[0]
            caim_ref[k, chan, :] = v[1]
            if k < S5_Q:
                lagged = pl.ds(k * gs + i, ng, stride=S5_Q * gs)
                ckre_ref[lagged, :] = v[0]
                ckim_ref[lagged, :] = v[1]


def _s5_disc(lam_re, lam_im, log_step, b_re_t, b_im_t, c_re_t, c_im_t):
    g, n = lam_re.shape
    gn = _sds((g, n))
    pq = _sds((SUBLANE, g, n))
    ba = _sds((S5_Q, g * S5_GSIZE, n))
    ca = _sds((S5_Q + 1, g * S5_GSIZE, n))
    ck = _sds((g * S5_Q * S5_GSIZE, n))
    return pl.pallas_call(
        _s5_disc_body,
        out_shape=[gn, gn, pq, pq, ba, ba, ca, ca, ck, ck],
        name="s5_disc",
    )(lam_re, lam_im, log_step, b_re_t, b_im_t, c_re_t, c_im_t)


def _hi_lo(a):
    hi = a.astype(BF16)
    return hi, (a - hi.astype(F32)).astype(BF16)


def _s5_kmat_body(bbr_ref, bbi_ref, ckr_ref, cki_ref, k_ref):
    dn = (((2,), (2,)), ((0,), (0,)))

    def mm(a, b):
        ah, al = _hi_lo(a)
        bh, bl = _hi_lo(b)
        d = lambda x, y: lax.dot_general(x, y, dn, preferred_element_type=F32)
        return d(ah, bh) + d(ah, bl) + d(al, bh)

    k_ref[...] = mm(bbr_ref[...], ckr_ref[...]) - mm(bbi_ref[...], cki_ref[...])


def _s5_kmat(bb_re, bb_im, ck_re, ck_im):
    g, m, _ = ck_re.shape
    return pl.pallas_call(
        _s5_kmat_body,
        out_shape=_sds((g, S5_GSIZE, m)),
        name="s5_kmat",
    )(bb_re, bb_im, ck_re, ck_im)


def _block_diag_lanes(a, width):
    rows, total = LANE, S5_LB * width
    rep = (lax.broadcasted_iota(jnp.int32, (width, total), 1) % width
           == lax.broadcasted_iota(jnp.int32, (width, total), 0)).astype(BF16)
    tiled = jnp.dot(a.astype(BF16), rep, preferred_element_type=F32)
    same_group = (lax.broadcasted_iota(jnp.int32, (rows, total), 0) // S5_GSIZE
                  == lax.broadcasted_iota(jnp.int32, (rows, total), 1) // width)
    return jnp.where(same_group, tiled, 0.0).astype(BF16)


def _s5_build_body(kt_ref, bar_ref, bai_ref, car_ref, cai_ref, tin_ref, wst_ref, wot_ref, c0t_ref):
    q = S5_Q
    gs = S5_GSIZE
    kt = kt_ref[...]
    lag_blocks = [_block_diag_lanes(kt[:, k * gs:(k + 1) * gs], gs) for k in range(q)]
    zero = jnp.zeros((LANE, LANE), BF16)
    for s in range(q):
        rs = slice(s * LANE, (s + 1) * LANE)
        for t in range(q):
            tin_ref[rs, t * LANE:(t + 1) * LANE] = lag_blocks[t - s] if t >= s else zero
        wst_ref[rs, 0:S5_LBN] = _block_diag_lanes(bar_ref[s], S5_STATE)
        wst_ref[rs, S5_LBN:] = _block_diag_lanes(bai_ref[s], S5_STATE)
        wot_ref[rs, 0:S5_LBN] = _block_diag_lanes(car_ref[s + 1], S5_STATE)
        wot_ref[rs, S5_LBN:] = _block_diag_lanes(-cai_ref[s + 1], S5_STATE)
    c0t_ref[:, 0:S5_LBN] = _block_diag_lanes(car_ref[0], S5_STATE)
    c0t_ref[:, S5_LBN:] = _block_diag_lanes(-cai_ref[0], S5_STATE)


def _s5_build(kt, ba_re, ba_im, ca_re, ca_im):
    nb, q = S5_GROUPS // S5_LB, S5_Q
    blk = lambda n: pl.BlockSpec((n, None, LANE, S5_STATE), lambda g: (0, g, 0, 0))
    big = pl.BlockSpec((None, q * LANE, 2 * S5_LBN), lambda g: (g, 0, 0))
    return pl.pallas_call(
        _s5_build_body,
        grid=(nb,),
        in_specs=[pl.BlockSpec((None, LANE, q * S5_GSIZE), lambda g: (g, 0, 0)), blk(q), blk(q), blk(q + 1), blk(q + 1)],
        out_specs=[pl.BlockSpec((None, q * LANE, q * LANE), lambda g: (g, 0, 0)), big, big,
                   pl.BlockSpec((None, LANE, 2 * S5_LBN), lambda g: (g, 0, 0))],
        out_shape=[_sds((nb, q * LANE, q * LANE), BF16), _sds((nb, q * LANE, 2 * S5_LBN), BF16),
                   _sds((nb, q * LANE, 2 * S5_LBN), BF16), _sds((nb, LANE, 2 * S5_LBN), BF16)],
        compiler_params=_cparams(1),
        name="s5_build",
    )(kt, ba_re, ba_im, ca_re, ca_im)


def _s5_operators(lam_re, lam_im, log_step, b_re, b_im, c_re, c_im):
    nb, q = S5_GROUPS // S5_LB, S5_Q
    g, gs, n = S5_GROUPS, S5_GSIZE, S5_STATE
    p1_re, p1_im, pq_re, pq_im, ba_re, ba_im, ca_re, ca_im, ck_re, ck_im = _s5_disc(
        lam_re, lam_im, log_step[:, None], b_re.transpose(2, 0, 1), b_im.transpose(2, 0, 1),
        c_re.transpose(1, 0, 2), c_im.transpose(1, 0, 2))
    kt = _s5_kmat(ba_re[q - 1].reshape(g, gs, n), ba_im[q - 1].reshape(g, gs, n),
                  ck_re.reshape(g, q * gs, n), ck_im.reshape(g, q * gs, n))
    by_block = lambda a: a.reshape(a.shape[0], nb, LANE, n)
    t_in, w_st, w_out_t, c_t = _s5_build(kt.reshape(nb, LANE, q * gs), by_block(ba_re), by_block(ba_im),
                                         by_block(ca_re), by_block(ca_im))

    lanes = lambda p: p.reshape(-1, nb, S5_LBN).transpose(1, 0, 2)
    rows = jnp.arange(SUBLANE)[None, :, None]
    pr, pi = lanes(pq_re), lanes(pq_im)
    tabs = [pr, pi]
    for d in (1, 2, 4):
        for p in (pr, pi):
            tabs.append(jnp.where(rows >= d, p[:, d - 1:d, :], 0.0))
    tab = jnp.stack(tabs, axis=1)
    abar = jnp.stack([lanes(p1_re), lanes(p1_im)], axis=1)
    return (t_in, w_st, w_out_t, tab), (w_st, c_t, abar)


def _s5_scan_tile(xr, xi, tab_ref, car_re, car_im):
    for d, k in ((1, 2), (2, 4), (4, 6)):
        ar = tab_ref[k]
        ai = tab_ref[k + 1]
        sr = pltpu.roll(xr, d, 0)
        si = pltpu.roll(xi, d, 0)
        xr, xi = xr + ar * sr - ai * si, xi + ar * si + ai * sr
    pr = tab_ref[0]
    pi = tab_ref[1]
    hr = xr + pr * car_re - pi * car_im
    hi = xi + pr * car_im + pi * car_re
    return hr, hi


def _s5_prompt_body(u_ref, tin_ref, wst_ref, wout_ref, tab_ref, d_ref, y_ref, hfin_ref, car, uf, yf):
    q = S5_Q
    j = pl.program_id(2)
    nj = pl.num_programs(2)
    nrow = u_ref.shape[0] // q

    @pl.when(j == 0)
    def _():
        car[...] = jnp.zeros(car.shape, F32)

    uf[...] = u_ref[...].astype(F32)
    xb = jnp.concatenate([uf[pl.ds(t, nrow, stride=q), :] for t in range(q)], axis=1).astype(BF16)
    s = jnp.dot(xb, wst_ref[...], preferred_element_type=F32)
    car_re = car[:, 0:S5_LBN]
    car_im = car[:, S5_LBN:]
    first = lax.broadcasted_iota(jnp.int32, (SUBLANE, S5_LBN), 0) == 0
    hp = []
    for t in range(nrow // SUBLANE):
        rs = slice(t * SUBLANE, (t + 1) * SUBLANE)
        hr, hi = _s5_scan_tile(s[rs, 0:S5_LBN], s[rs, S5_LBN:], tab_ref, car_re, car_im)
        hp.append(jnp.concatenate([jnp.where(first, car_re, pltpu.roll(hr, 1, 0)),
                                   jnp.where(first, car_im, pltpu.roll(hi, 1, 0))], axis=1))
        car_re = jnp.broadcast_to(hr[SUBLANE - 1:SUBLANE, :], hr.shape)
        car_im = jnp.broadcast_to(hi[SUBLANE - 1:SUBLANE, :], hi.shape)
    car[:, 0:S5_LBN] = car_re
    car[:, S5_LBN:] = car_im
    h_in = jnp.concatenate(hp, axis=0).astype(BF16)
    d_row = jnp.concatenate([d_ref[...]] * q, axis=1)
    y = (jnp.dot(xb, tin_ref[...], preferred_element_type=F32)
         + lax.dot_general(h_in, wout_ref[...], (((1,), (1,)), ((), ())), preferred_element_type=F32)
         + d_row * xb.astype(F32))
    y = _gelu_tanh(y)
    for t in range(q):
        yf[pl.ds(t, nrow, stride=q), :] = y[:, t * LANE:(t + 1) * LANE]
    y_ref[...] = yf[...].astype(y_ref.dtype)

    @pl.when(j == nj - 1)
    def _():
        hfin_ref[...] = car[...]


def _s5_prompt(proj, t_in, w_st, w_out, tab, d_skip, ntok):
    b, l, cols = proj.shape
    q = S5_Q
    nb = S5_GROUPS // S5_LB
    ub = P_U // LANE
    wmap = lambda g, i, j: (g, 0, 0)
    return pl.pallas_call(
        _s5_prompt_body,
        grid=(nb, b, l // ntok),
        in_specs=[pl.BlockSpec((None, ntok, LANE), lambda g, i, j: (i, j, ub + g)),
                  pl.BlockSpec((None, q * LANE, q * LANE), wmap),
                  pl.BlockSpec((None, q * LANE, 2 * S5_LBN), wmap),
                  pl.BlockSpec((None, 2 * S5_LBN, q * LANE), wmap),
                  pl.BlockSpec((None, 8, SUBLANE, S5_LBN), lambda g, i, j: (g, 0, 0, 0)),
                  pl.BlockSpec((1, LANE), lambda g, i, j: (0, g))],
        out_specs=[pl.BlockSpec((None, None, ntok, LANE), lambda g, i, j: (g, i, j, 0)),
                   pl.BlockSpec((None, None, SUBLANE, 2 * S5_LBN), lambda g, i, j: (i, g, 0, 0))],
        out_shape=[_sds((nb, b, l, LANE), BF16), _sds((b, nb, SUBLANE, 2 * S5_LBN))],
        scratch_shapes=[pltpu.VMEM((SUBLANE, 2 * S5_LBN), F32), pltpu.VMEM((ntok, LANE), F32),
                        pltpu.VMEM((ntok, LANE), F32)],
        compiler_params=_cparams(3),
        name="s5_prompt",
    )(proj, t_in, w_st, w_out, tab, d_skip)


def _rope_tables(pos):
    half = ROT_DIM // 2
    inv_freq = jnp.exp(-(2.0 * jnp.arange(half, dtype=F32) / ROT_DIM) * math.log(ROPE_THETA))
    ang = pos.astype(F32)[:, None] * inv_freq[None, :]
    cos, sin = jnp.cos(ang), jnp.sin(ang)
    l = pos.shape[0]
    one = jnp.ones((l, HEAD_DIM - ROT_DIM), F32)
    zero = jnp.zeros((l, HEAD_DIM - ROT_DIM), F32)
    zh = jnp.zeros((l, half), F32)
    cos_h = jnp.concatenate([cos, cos, one], axis=1)
    sa_h = jnp.concatenate([-sin, zh, zero], axis=1)
    sb_h = jnp.concatenate([zh, sin, zero], axis=1)
    two = lambda t: jnp.concatenate([t, t], axis=1)
    return two(cos_h), two(sa_h), two(sb_h)


def _rope_flat(x, cos, sa, sb):
    w = x.shape[1]
    n = w // LANE
    tile = lambda t: jnp.concatenate([t] * n, axis=1) if n > 1 else t
    half = ROT_DIM // 2
    return x * tile(cos) + pltpu.roll(x, w - half, 1) * tile(sa) + pltpu.roll(x, half, 1) * tile(sb)


def _attn_prompt_body(q_ref, k_ref, v_ref, cos_ref, sa_ref, sb_ref, sink_ref, wo_ref, y_ref, kn_ref, vn_ref,
                      kprev, vprev):
    j = pl.program_id(1)
    nb = pl.num_programs(1)
    w = WINDOW
    kvw = KV_HEADS * HEAD_DIM

    @pl.when(j == 0)
    def _():
        kprev[...] = jnp.zeros(kprev.shape, F32)
        vprev[...] = jnp.zeros(vprev.shape, F32)

    cos, sa, sb = cos_ref[...], sa_ref[...], sb_ref[...]
    q = _rope_flat(q_ref[...].astype(F32), cos, sa, sb) * (HEAD_DIM ** -0.5)
    k = _rope_flat(k_ref[...].astype(F32), cos, sa, sb)
    v = v_ref[...].astype(F32)
    kctx = jnp.concatenate([kprev[...], k], axis=0).astype(BF16)
    vt = jnp.concatenate([vprev[...], v], axis=0).T.astype(BF16)
    rows = Q_PER_KV * w
    ci = lax.broadcasted_iota(jnp.int32, (2 * w, rows), 0)
    ti = lax.broadcasted_iota(jnp.int32, (2 * w, rows), 1) % w
    valid = (ci >= ti) & (ci <= ti + w) & ((ci >= w) | (j > 0))
    qb = q.astype(BF16)
    outs = []
    for kv in range(KV_HEADS):
        kj = kctx[:, kv * HEAD_DIM:(kv + 1) * HEAD_DIM]
        vjt = vt[kv * HEAD_DIM:(kv + 1) * HEAD_DIM, :]
        heads = range(kv * Q_PER_KV, (kv + 1) * Q_PER_KV)
        qs = jnp.concatenate([qb[:, h * HEAD_DIM:(h + 1) * HEAD_DIM] for h in heads], axis=0)
        sink = jnp.concatenate([jnp.broadcast_to(sink_ref[0:1, h:h + 1], (1, w)) for h in heads], axis=1)
        s = lax.dot_general(kj, qs, (((1,), (1,)), ((), ())), preferred_element_type=F32)
        s = jnp.where(valid, s, -jnp.inf)
        m = jnp.maximum(jnp.max(s, axis=0, keepdims=True), sink)
        p = jnp.exp(s - m).astype(BF16)
        den = jnp.sum(p.astype(F32), axis=0, keepdims=True) + jnp.exp(sink - m)
        o_t = jnp.dot(vjt, p, preferred_element_type=F32) / den
        outs.extend(o_t[:, i * w:(i + 1) * w].T for i in range(Q_PER_KV))
    o = jnp.concatenate(outs, axis=1)
    y_ref[...] = jnp.dot(o.astype(BF16), wo_ref[...], preferred_element_type=F32).astype(y_ref.dtype)
    kprev[...] = k
    vprev[...] = v

    @pl.when(j == nb - 1)
    def _():
        kn_ref[...] = k
        vn_ref[...] = v


def _attn_prompt(proj, cos, sa, sb, sinks, attn_o):
    b, l, _ = proj.shape
    d = D_MODEL
    w = WINDOW
    kvw = KV_HEADS * HEAD_DIM
    const2 = lambda i, j: (0, 0)
    tmap = lambda i, j: (j, 0)
    return pl.pallas_call(
        _attn_prompt_body,
        grid=(b, l // w),
        in_specs=[pl.BlockSpec((None, w, d), lambda i, j: (i, j, P_Q // d)),
                  pl.BlockSpec((None, w, kvw), lambda i, j: (i, j, P_K // kvw)),
                  pl.BlockSpec((None, w, kvw), lambda i, j: (i, j, P_V // kvw)),
                  pl.BlockSpec((w, LANE), tmap), pl.BlockSpec((w, LANE), tmap), pl.BlockSpec((w, LANE), tmap),
                  pl.BlockSpec((1, LANE), const2),
                  pl.BlockSpec((d, d), const2)],
        out_specs=[pl.BlockSpec((None, w, d), lambda i, j: (i, j, 0)),
                   pl.BlockSpec((None, w, kvw), lambda i, j: (i, 0, 0)),
                   pl.BlockSpec((None, w, kvw), lambda i, j: (i, 0, 0))],
        out_shape=[_sds((b, l, d), BF16), _sds((b, w, kvw)), _sds((b, w, kvw))],
        scratch_shapes=[pltpu.VMEM((w, kvw), F32), pltpu.VMEM((w, kvw), F32)],
        compiler_params=_cparams(2),
        name="attn_prompt",
    )(proj, proj, proj, cos, sa, sb, sinks, attn_o)


SAMPLE_TB = 8


def _mamba_sample_body(x_ref, bc_ref, dt_ref, cst_ref, ssm_ref, cw_ref, cb_ref, dtb_ref, alog_ref, dskip_ref,
                       exp_ref, stack_ref, y_ref, cnew_ref, ssmo_ref):
    del stack_ref
    tb = SAMPLE_TB
    new = jnp.concatenate([x_ref[...], bc_ref[...]], axis=1)
    conv = (cb_ref[...] + cw_ref[0:1, :] * cst_ref[0] + cw_ref[1:2, :] * cst_ref[1] + cw_ref[2:3, :] * cst_ref[2]
            + cw_ref[3:4, :] * new)
    cnew_ref[0] = cst_ref[1]
    cnew_ref[1] = cst_ref[2]
    cnew_ref[2] = new
    xbc = _silu(conv)
    xs = xbc[:, 0:M_INNER]
    gn = M_GROUPS * M_STATE
    bm = xbc[:, M_INNER:M_INNER + gn]
    cm = xbc[:, M_INNER + gn:].astype(BF16)
    dt = _softplus(dt_ref[...] + dtb_ref[...])
    a = dt * (-jnp.exp(alog_ref[...]))
    expand = exp_ref[...]
    dtx = _dot3_right(dt, expand) * xs
    da_x = jnp.exp(_dot3_right(a, expand))
    nl = M_INNER // LANE
    slab = jnp.concatenate([dtx[:, j * LANE:(j + 1) * LANE] for j in range(nl)]
                           + [da_x[:, j * LANE:(j + 1) * LANE] for j in range(nl)], axis=0)
    tt = slab.T
    hpg = M_HEADS // M_GROUPS
    gw = hpg * M_HEADDIM
    rowid = lax.broadcasted_iota(jnp.int32, (tb, gw), 0)
    ys = []
    for g in range(M_GROUPS):
        yacc = jnp.zeros((tb, gw), F32)
        for t in range(tb):
            parts = []
            for hl in range(hpg):
                h = g * hpg + hl
                j, hh = divmod(h * M_HEADDIM, LANE)
                col = j * tb + t
                dcol = tt[hh:hh + M_HEADDIM, col:col + 1]
                acol = tt[hh:hh + M_HEADDIM, nl * tb + col:nl * tb + col + 1]
                hn = acol * ssm_ref[t, h] + dcol * bm[t:t + 1, g * M_STATE:(g + 1) * M_STATE]
                ssmo_ref[t, h] = hn
                parts.append(hn)
            hng = jnp.concatenate(parts, axis=0).astype(BF16)
            yg = lax.dot_general(cm[:, g * M_STATE:(g + 1) * M_STATE], hng, (((1,), (1,)), ((), ())),
                                 preferred_element_type=F32)
            yacc = jnp.where(rowid == t, yg, yacc)
        ys.append(yacc)
    y_ref[...] = jnp.concatenate(ys, axis=1) + xs * dskip_ref[...]


def _mamba_sample(proj, dtp, conv_t, ssm, conv_w, conv_b, dt_bias, a_log, d_skip, expand, ssm_stack, li):
    s = proj.shape[0]
    d = D_MODEL
    tb = SAMPLE_TB
    const = lambda i: (0, 0)
    return pl.pallas_call(
        _mamba_sample_body,
        grid=(s // tb,),
        in_specs=[pl.BlockSpec((tb, d), lambda i: (i, S_X // d)),
                  pl.BlockSpec((tb, d), lambda i: (i, S_BC // d)),
                  pl.BlockSpec((tb, LANE), lambda i: (i, 0)),
                  pl.BlockSpec((CONV_W - 1, tb, CONV_DIM), lambda i: (0, i, 0)),
                  pl.BlockSpec((None, tb, M_HEADS, M_HEADDIM, M_STATE), lambda i: (li, i, 0, 0, 0)),
                  pl.BlockSpec((CONV_W, CONV_DIM), const), pl.BlockSpec((1, CONV_DIM), const),
                  pl.BlockSpec((1, LANE), const), pl.BlockSpec((1, LANE), const), pl.BlockSpec((1, d), const),
                  pl.BlockSpec((LANE, d), const),
                  pl.BlockSpec(memory_space=pl.ANY)],
        out_specs=[pl.BlockSpec((tb, d), lambda i: (i, 0)),
                   pl.BlockSpec((CONV_W - 1, tb, CONV_DIM), lambda i: (0, i, 0)),
                   pl.BlockSpec((None, tb, M_HEADS, M_HEADDIM, M_STATE), lambda i: (li, i, 0, 0, 0))],
        out_shape=[_sds((s, d)), _sds((CONV_W - 1, s, CONV_DIM)), _sds(ssm_stack.shape)],
        input_output_aliases={11: 2},
        compiler_params=_cparams(1),
        name="mamba_sample",
    )(proj, proj, dtp, conv_t, ssm, conv_w, conv_b, dt_bias, a_log, d_skip, expand, ssm_stack)


def _gate_norm_proj_body(y_ref, z_ref, nw_ref, w_ref, o_ref):
    y = _group_norm_gate(y_ref[...], z_ref[...], nw_ref[...])
    o_ref[...] = jnp.dot(y.astype(BF16), w_ref[...], preferred_element_type=F32)


def _gate_norm_proj(y, proj, m_norm_w, m_proj):
    s, d = y.shape
    const = lambda i: (0, 0)
    return pl.pallas_call(
        _gate_norm_proj_body,
        grid=(1,),
        in_specs=[pl.BlockSpec((s, d), const), pl.BlockSpec((s, d), lambda i: (0, S_Z // d)),
                  pl.BlockSpec((1, d), const), pl.BlockSpec((d, d), const)],
        out_specs=pl.BlockSpec((s, d), const),
        out_shape=_sds((s, d)),
        compiler_params=_cparams(1),
        name="gate_norm_proj",
    )(y, proj, m_norm_w, m_proj)


def _s5_sample_body(u_ref, hre_ref, him_ref, wb_ref, ct_ref, tab_ref, d_ref, y_ref, ore_ref, oim_ref):
    u = u_ref[...]
    bu = jnp.dot(u.astype(BF16), wb_ref[...], preferred_element_type=F32)
    bre = bu[:, 0:S5_LBN]
    bim = bu[:, S5_LBN:]
    ar = tab_ref[0]
    ai = tab_ref[1]
    h0r = hre_ref[...]
    h0i = him_ref[...]
    hr = ar * h0r - ai * h0i + bre
    hi = ar * h0i + ai * h0r + bim
    ore_ref[...] = hr
    oim_ref[...] = hi
    h = jnp.concatenate([hr, hi], axis=1).astype(BF16)
    y = lax.dot_general(h, ct_ref[...], (((1,), (1,)), ((), ())), preferred_element_type=F32) + d_ref[...] * u
    y_ref[...] = _gelu_tanh(y)


def _s5_sample(proj, h_re, h_im, w_st, c_t, tab, d_skip):
    s = proj.shape[0]
    nb = S5_GROUPS // S5_LB
    ub = S_U // LANE
    wmap = lambda g: (g, 0, 0)
    hspec = pl.BlockSpec((s, S5_LBN), lambda g: (0, g))
    return pl.pallas_call(
        _s5_sample_body,
        grid=(nb,),
        in_specs=[pl.BlockSpec((s, LANE), lambda g: (0, ub + g)), hspec, hspec,
                  pl.BlockSpec((None, LANE, 2 * S5_LBN), lambda g: (g, S5_Q - 1, 0)),
                  pl.BlockSpec((None, LANE, 2 * S5_LBN), wmap),
                  pl.BlockSpec((None, 2, 1, S5_LBN), lambda g: (g, 0, 0, 0)),
                  pl.BlockSpec((1, LANE), lambda g: (0, g))],
        out_specs=[pl.BlockSpec((None, s, LANE), wmap), hspec, hspec],
        out_shape=[_sds((nb, s, LANE)), _sds(h_re.shape), _sds(h_im.shape)],
        compiler_params=_cparams(1),
        name="s5_sample",
    )(proj, h_re, h_im, w_st, c_t, tab, d_skip)


def _rope_sample_body(q_ref, k_ref, cos_ref, sa_ref, sb_ref, qo_ref, ko_ref):
    cos, sa, sb = cos_ref[...], sa_ref[...], sb_ref[...]
    qo_ref[...] = _rope_flat(q_ref[...], cos, sa, sb) * (HEAD_DIM ** -0.5)
    ko_ref[...] = _rope_flat(k_ref[...], cos, sa, sb)


def _rope_sample(proj, cos, sa, sb):
    s = proj.shape[0]
    kvw = KV_HEADS * HEAD_DIM
    const = lambda i: (0, 0)
    return pl.pallas_call(
        _rope_sample_body,
        grid=(1,),
        in_specs=[pl.BlockSpec((s, QX), lambda i: (0, S_QX // QX)), pl.BlockSpec((s, kvw), lambda i: (0, S_K // kvw)),
                  pl.BlockSpec((1, LANE), const), pl.BlockSpec((1, LANE), const), pl.BlockSpec((1, LANE), const)],
        out_specs=[pl.BlockSpec((s, QX), const), pl.BlockSpec((s, kvw), const)],
        out_shape=[_sds((s, QX)), _sds((s, kvw))],
        compiler_params=_cparams(1),
        name="rope_sample",
    )(proj, proj, cos, sa, sb)


def _attn_sample_body(q_ref, kc_ref, vc_ref, kn_ref, vn_ref, sink_ref, kstack_ref, vstack_ref, o_ref, ko_ref, vo_ref):
    del kstack_ref, vstack_ref
    q = q_ref[...]
    kc = kc_ref[...]
    vc = vc_ref[...]
    kn = kn_ref[...]
    vn = vn_ref[...]
    w = kc.shape[1]
    s_c = lax.dot_general(q.astype(BF16), kc.astype(BF16), (((2,), (2,)), ((0,), (0,))),
                          preferred_element_type=F32)
    s_n = jnp.sum(q * kn, axis=-1, keepdims=True)
    sink = sink_ref[...][None]
    m = jnp.maximum(jnp.maximum(jnp.max(s_c, axis=-1, keepdims=True), s_n), sink)
    p_c = jnp.exp(s_c - m)
    p_n = jnp.exp(s_n - m)
    den = jnp.sum(p_c, axis=-1, keepdims=True) + p_n + jnp.exp(sink - m)
    o = lax.dot_general((p_c / den).astype(BF16), vc.astype(BF16), (((2,), (1,)), ((0,), (0,))),
                        preferred_element_type=F32)
    o_ref[...] = o + (p_n / den) * vn
    ko_ref[:, 0:w - 1, :] = kc[:, 1:w, :]
    ko_ref[:, w - 1:w, :] = kn
    vo_ref[:, 0:w - 1, :] = vc[:, 1:w, :]
    vo_ref[:, w - 1:w, :] = vn


def _attn_sample(qx, kc, vc, kn, proj3, sinks, k_stack, v_stack, li):
    s, w, kvw = kc.shape
    tb = SAMPLE_TB
    row3 = lambda i: (i, 0, 0)
    slab = pl.BlockSpec((None, tb, w, kvw), lambda i: (li, i, 0, 0))
    return pl.pallas_call(
        _attn_sample_body,
        grid=(s // tb,),
        in_specs=[pl.BlockSpec((tb, A_HEADS, kvw), row3), pl.BlockSpec((tb, w, kvw), row3),
                  pl.BlockSpec((tb, w, kvw), row3), pl.BlockSpec((tb, 1, kvw), row3),
                  pl.BlockSpec((tb, 1, kvw), lambda i: (i, 0, S_V // kvw)),
                  pl.BlockSpec((A_HEADS, 1), lambda i: (0, 0)),
                  pl.BlockSpec(memory_space=pl.ANY), pl.BlockSpec(memory_space=pl.ANY)],
        out_specs=[pl.BlockSpec((tb, A_HEADS, kvw), row3), slab, slab],
        out_shape=[_sds((s, A_HEADS, kvw)), _sds(k_stack.shape), _sds(v_stack.shape)],
        input_output_aliases={6: 1, 7: 2},
        compiler_params=_cparams(1),
        name="attn_sample",
    )(qx, kc, vc, kn, proj3, sinks, k_stack, v_stack)


def _dense_body(x_ref, w_ref, o_ref):
    o_ref[...] = jnp.dot(x_ref[...].astype(BF16), w_ref[...], preferred_element_type=F32)


def _dense(x, w, tn):
    m, k = x.shape
    n = w.shape[1]
    return pl.pallas_call(
        _dense_body,
        grid=(n // tn,),
        in_specs=[pl.BlockSpec((m, k), lambda j: (0, 0)), pl.BlockSpec((k, tn), lambda j: (0, j))],
        out_specs=pl.BlockSpec((m, tn), lambda j: (0, j)),
        out_shape=_sds((m, n)),
        compiler_params=_cparams(1),
        name="dense",
    )(x, w)


def _pad_lanes(v):
    return jnp.zeros((1, LANE), F32).at[0, :v.shape[0]].set(v.astype(F32))


def _projection_weights(w_in, attn_o):
    depth = w_in.shape[0]
    w = w_in.astype(BF16)
    seg = lambda a, b: w[:, :, a:b]
    z, xbc, dt, u = seg(OFF_Z, OFF_XBC), seg(OFF_XBC, OFF_DT), seg(OFF_DT, OFF_U), seg(OFF_U, OFF_Q)
    q, k, v, g = seg(OFF_Q, OFF_K), seg(OFF_K, OFF_V), seg(OFF_V, OFF_G), seg(OFF_G, IN_COLS)
    w_prompt = jnp.concatenate([z, xbc, u, q, g, k, v], axis=2)
    sel = (jnp.arange(A_HEADS)[:, None] // Q_PER_KV == jnp.arange(KV_HEADS)[None, :]).astype(BF16)
    qx = (q.reshape(depth, D_MODEL, A_HEADS, 1, HEAD_DIM) * sel[None, None, :, :, None]).reshape(depth, D_MODEL, QX)
    w_sample = jnp.concatenate([z, xbc, u, qx, g, k, v], axis=2)
    w_dt = jnp.pad(dt, ((0, 0), (0, 0), (0, LANE - M_HEADS)))
    ao = attn_o.astype(BF16).reshape(depth, A_HEADS, 1, HEAD_DIM, D_MODEL) * sel[None, :, :, None, None]
    return w_prompt, w_sample, w_dt, ao.reshape(depth, QX, D_MODEL)


def kernel(x_prompt, x_sample, state_ssm, state_conv, state_s5_re, state_s5_im, cache_k, cache_v, norm1_w, w_in,
           conv_w, conv_b, dt_bias, a_log, m_d, m_norm_w, m_proj, s5_lam_re, s5_lam_im, s5_log_step, s5_b_re,
           s5_b_im, s5_c_re, s5_c_im, s5_d, s5_glu_w, attn_sinks, attn_o, w_out, norm2_w, mlp_up, mlp_down,
           final_norm_w):
    b, l, d = x_prompt.shape
    s = x_sample.shape[0]
    kvw = KV_HEADS * HEAD_DIM
    nb = S5_GROUPS // S5_LB
    xp = x_prompt.reshape(b * l, d)
    xs = x_sample.reshape(s, d)
    cos_p, sa_p, sb_p = _rope_tables(jnp.arange(l, dtype=jnp.int32))
    cos_s, sa_s, sb_s = _rope_tables(jnp.full((1,), PAST_LEN, jnp.int32))
    expand = (jnp.arange(LANE)[:, None] == jnp.arange(M_INNER)[None, :] // M_HEADDIM).astype(BF16)
    fnw = final_norm_w[None]
    w_prompt_all, w_sample_all, w_dt_all, ao_x_all = _projection_weights(w_in, attn_o)
    ssm_s = jnp.zeros((DEPTH,) + state_ssm.shape[1:], F32)
    k_s = jnp.zeros((DEPTH, s, WINDOW, kvw), F32)
    v_s = jnp.zeros((DEPTH, s, WINDOW, kvw), F32)
    outs = [[] for _ in range(9)]
    for li in range(DEPTH):
        final = li == DEPTH - 1
        w_prompt, w_sample, w_dt, ao_x = w_prompt_all[li], w_sample_all[li], w_dt_all[li], ao_x_all[li]
        n1 = norm1_w[li][None]
        cw, cb = conv_w[li], conv_b[li][None]
        dtb, alog = _pad_lanes(dt_bias[li]), _pad_lanes(a_log[li])
        dsk = jnp.repeat(m_d[li], M_HEADDIM)[None]
        mnw = m_norm_w[li][None]
        mpj = m_proj[li].astype(BF16)
        s5_chunked, s5_step = _s5_operators(s5_lam_re[li], s5_lam_im[li], s5_log_step[li], s5_b_re[li], s5_b_im[li],
                                            s5_c_re[li], s5_c_im[li])
        s5d = s5_d[li][None]
        sinks = attn_sinks[li]
        ao = attn_o[li].astype(BF16)
        mlp_w = (s5_glu_w[li].astype(BF16), w_out[li].astype(BF16), norm2_w[li][None], mlp_up[li].astype(BF16),
                 mlp_down[li].astype(BF16), fnw)

        proj, dtp = _norm_proj(xp, n1, w_prompt, w_dt, 2048, 512, BF16)
        proj3 = proj.reshape(b, l, P_COLS)
        ym, conv_p, ssm_p = _mamba_prompt(proj3, dtp.reshape(b, l, LANE), cw, cb, dtb, alog, dsk, mnw, expand, mpj)
        ys, h_p = _s5_prompt(proj3, *s5_chunked, s5d, 4096)
        hre_p, him_p = h_p[:, :, 0, :S5_LBN], h_p[:, :, 0, S5_LBN:]
        ya, k_p, v_p = _attn_prompt(proj3, cos_p, sa_p, sb_p, _pad_lanes(sinks), ao)
        xp = _merge_mlp(xp, proj, P_G, ym.reshape(b * l, d), ys.reshape(nb, b * l, LANE), ya.reshape(b * l, d),
                        *mlp_w, 512, final)

        sproj, sdt = _norm_proj(xs, n1, w_sample, w_dt, s, 512, F32)
        y_pre, conv_t, ssm_s = _mamba_sample(sproj, sdt, state_conv[li].transpose(1, 0, 2), state_ssm, cw, cb,
                                             dtb, alog, dsk, expand, ssm_s, li)
        sym = _gate_norm_proj(y_pre, sproj, mnw, mpj)
        sys_, hre_s, him_s = _s5_sample(sproj, state_s5_re[li].reshape(s, S5_GROUPS * S5_STATE),
                                        state_s5_im[li].reshape(s, S5_GROUPS * S5_STATE), *s5_step, s5d)
        qx_rot, k_rot = _rope_sample(sproj, cos_s, sa_s, sb_s)
        o, k_s, v_s = _attn_sample(qx_rot.reshape(s, A_HEADS, kvw), cache_k[li].reshape(s, WINDOW, kvw),
                                   cache_v[li].reshape(s, WINDOW, kvw), k_rot.reshape(s, 1, kvw),
                                   sproj.reshape(s, 1, S_COLS), sinks[:, None], k_s, v_s, li)
        sya = _dense(o.reshape(s, QX), ao_x, 512)
        xs = _merge_mlp(xs, sproj, S_G, sym, sys_, sya, *mlp_w, s, final)

        for i, val in enumerate((
                ssm_p, conv_p, conv_t.transpose(1, 0, 2),
                hre_p.reshape(b, S5_GROUPS, S5_STATE), hre_s.reshape(s, S5_GROUPS, S5_STATE),
                him_p.reshape(b, S5_GROUPS, S5_STATE), him_s.reshape(s, S5_GROUPS, S5_STATE),
                k_p.reshape(b, WINDOW, KV_HEADS, HEAD_DIM), v_p.reshape(b, WINDOW, KV_HEADS, HEAD_DIM))):
            outs[i].append(val)
    ssm_p, conv_p, conv_s, hre_p, hre_s, him_p, him_s, k_p, v_p = (jnp.stack(o) for o in outs)
    cache_shape = (DEPTH, s, WINDOW, KV_HEADS, HEAD_DIM)
    return (xp.reshape(b, l, d), xs.reshape(s, 1, d), ssm_p, ssm_s, conv_p, conv_s, hre_p, hre_s, him_p, him_s,
            k_p, k_s.reshape(cache_shape), v_p, v_s.reshape(cache_shape))
```
